```python
import math, functools
import jax, jax.numpy as jnp
from jax import lax
import numpy as np

D_MODEL = 1024
BATCH = 16
SEQ = 2048
DEPTH = 2
DEC_BATCH = 32
DEC_SEQ = 8
PAST_LEN = 16384
PAGE_SIZE = 128

ATTN_WIDTH = D_MODEL // 2
SSM_WIDTH = D_MODEL - ATTN_WIDTH
HEAD_DIM = 64
N_HEADS = ATTN_WIDTH // HEAD_DIM
SSM_GROUP = 16
N_SSM_GROUPS = SSM_WIDTH // SSM_GROUP
SSM_STATE = 64
D_FF = -(-8 * D_MODEL // (3 * 256)) * 256
PLE_DIM = 256
Q_BLOCK = 128
PROJ_WIDTH = 3 * ATTN_WIDTH + N_HEADS + SSM_WIDTH
RMS_EPS = 1e-6
NEG_INF = -1e30
CACHED_FORGET_LOGIT = 9.0

kernel_name = 'fox_s5_hymba_decode_step'


def rmsnorm(x, g):
    x32 = x.astype(jnp.float32)
    y = x32 * lax.rsqrt(jnp.mean(x32 * x32, axis=-1, keepdims=True) + RMS_EPS) * g.astype(jnp.float32)
    return y.astype(x.dtype)


def _ssm_combine(left, right):
    a_l, b_l = left
    a_r, b_r = right
    return a_l * a_r, a_r * b_l + b_r


def s5_mix(u, h0, lam_re, lam_im, log_dt, b_re, b_im, c_re, c_im, d_skip, w_glu, b_glu):
    f32 = jnp.float32
    bsz, seqlen, _ = u.shape
    ug = u.astype(f32).reshape(bsz, seqlen, N_SSM_GROUPS, SSM_GROUP)
    lam = lax.complex(lam_re.astype(f32), lam_im.astype(f32))
    dt = jnp.exp(log_dt.astype(f32))[:, None]
    lam_bar = jnp.exp(lam * dt)
    b_bar = ((lam_bar - 1.0) / lam)[:, :, None] * lax.complex(b_re.astype(f32), b_im.astype(f32))
    bu = jnp.einsum('blgh,gph->blgp', ug.astype(jnp.complex64), b_bar)
    if h0 is not None:
        bu = bu.at[:, 0].add(lam_bar * h0)
    a = jnp.broadcast_to(lam_bar, bu.shape)
    _, states = lax.associative_scan(_ssm_combine, (a, bu), axis=1)
    c = lax.complex(c_re.astype(f32), c_im.astype(f32))
    y = jnp.einsum('blgp,ghp->blgh', states, c).real + d_skip.astype(f32) * ug
    y = jax.nn.gelu(y.reshape(bsz, seqlen, SSM_WIDTH))
    out = y * jax.nn.sigmoid(y @ w_glu.astype(f32) + b_glu.astype(f32))
    return out.astype(u.dtype), states[:, -1]


def fox_attend_prompt(q, k, v, logf):
    scale = HEAD_DIM ** -0.5
    seqlen = q.shape[1]
    c = jnp.cumsum(logf, axis=1).transpose(0, 2, 1)
    outs = []
    for start in range(0, seqlen, Q_BLOCK):
        end = min(start + Q_BLOCK, seqlen)
        s = jnp.einsum('bqhd,bkhd->bhqk', q[:, start:end], k[:, :end],
                       preferred_element_type=jnp.float32) * scale
        s = s + c[:, :, start:end, None] - c[:, :, None, :end]
        causal = jnp.arange(start, end)[:, None] >= jnp.arange(end)[None, :]
        s = jnp.where(causal, s, NEG_INF)
        probs = jax.nn.softmax(s, axis=-1).astype(v.dtype)
        outs.append(jnp.einsum('bhqk,bkhd->bqhd', probs, v[:, :end]))
    return jnp.concatenate(outs, axis=1)


def fox_attend_sample(q, k, v, logf, k_past, v_past, logf_past):
    f32 = jnp.float32
    scale = HEAD_DIM ** -0.5
    t = q.shape[1]
    n_past = k_past.shape[1]
    c_new = jnp.cumsum(logf, axis=1).transpose(0, 2, 1)
    lp = logf_past.astype(f32)
    r_past = (lax.cumsum(lp, axis=1, reverse=True) - lp).transpose(0, 2, 1)
    s_past = jnp.einsum('bqhd,bkhd->bhqk', q, k_past, preferred_element_type=f32) * scale
    s_past = s_past + c_new[:, :, :, None] + r_past[:, :, None, :]
    s_new = jnp.einsum('bqhd,bkhd->bhqk', q, k, preferred_element_type=f32) * scale
    s_new = s_new + c_new[:, :, :, None] - c_new[:, :, None, :]
    s_new = jnp.where(jnp.tril(jnp.ones((t, t), dtype=bool)), s_new, NEG_INF)
    probs = jax.nn.softmax(jnp.concatenate([s_past, s_new], axis=-1), axis=-1).astype(v.dtype)
    out = jnp.einsum('bhqk,bkhd->bqhd', probs[..., :n_past], v_past)
    return out + jnp.einsum('bhqk,bkhd->bqhd', probs[..., n_past:], v)


def hybrid_layer(h, p_l, attend, h0, lw):
    bsz, seqlen, _ = h.shape
    n = rmsnorm(h, lw['g_mix'])
    proj = n @ lw['w_in']
    a0, a1, a2, a3 = ATTN_WIDTH, 2 * ATTN_WIDTH, 3 * ATTN_WIDTH, 3 * ATTN_WIDTH + N_HEADS
    q = rmsnorm(proj[..., :a0].reshape(bsz, seqlen, N_HEADS, HEAD_DIM), lw['g_q'])
    k = rmsnorm(proj[..., a0:a1].reshape(bsz, seqlen, N_HEADS, HEAD_DIM), lw['g_k'])
    v = proj[..., a1:a2].reshape(bsz, seqlen, N_HEADS, HEAD_DIM)
    logf = jax.nn.log_sigmoid(proj[..., a2:a3].astype(jnp.float32) + lw['b_f'].astype(jnp.float32))
    u = proj[..., a3:]
    attn = attend(q, k, v, logf).reshape(bsz, seqlen, ATTN_WIDTH)
    ssm, h_last = s5_mix(u, h0, lw['lam_re'], lw['lam_im'], lw['log_dt'], lw['b_re'], lw['b_im'],
                         lw['c_re'], lw['c_im'], lw['d_skip'], lw['w_glu'], lw['b_glu'])
    mixed = jnp.concatenate([rmsnorm(attn, lw['g_attn_out']), rmsnorm(ssm, lw['g_ssm_out'])], axis=-1)
    h = h + mixed @ lw['w_out']
    n2 = rmsnorm(h, lw['g_ffn'])
    h = h + (jax.nn.silu(n2 @ lw['w_ff1']) * (n2 @ lw['w_ff3'])) @ lw['w_ff2']
    gate = jax.nn.sigmoid(rmsnorm(h, lw['g_ple']) @ lw['w_pg'] + lw['b_pg'])
    h = h + gate * (p_l @ lw['w_pe'])
    return h, k, v, logf, h_last


def setup_inputs(seed: int = 0) -> dict:
    key = jax.random.key(seed)
    ks = iter(jax.random.split(key, 48))

    def nrm(shape, scale):
        return scale * jax.random.normal(next(ks), shape, jnp.float32)

    def gain(shape):
        return 1.0 + nrm(shape, 0.02)

    n_pages = PAST_LEN // PAGE_SIZE
    n_used = DEC_BATCH * n_pages
    n_pool = n_used + max(1, n_used // 4)
    G, P = N_SSM_GROUPS, SSM_STATE
    x_prompt = jax.random.normal(next(ks), (BATCH, SEQ, D_MODEL), jnp.float32)
    x_sample = jax.random.normal(next(ks), (DEC_BATCH, DEC_SEQ, D_MODEL), jnp.float32)
    p_prompt = jax.random.normal(next(ks), (DEPTH, BATCH, SEQ, PLE_DIM), jnp.float32)
    p_sample = jax.random.normal(next(ks), (DEPTH, DEC_BATCH, DEC_SEQ, PLE_DIM), jnp.float32)
    cache_k = jax.random.normal(next(ks), (DEPTH, n_pool, PAGE_SIZE, N_HEADS, HEAD_DIM), jnp.float32)
    cache_v = jax.random.normal(next(ks), (DEPTH, n_pool, PAGE_SIZE, N_HEADS, HEAD_DIM), jnp.float32)
    cache_logf = jax.nn.log_sigmoid(CACHED_FORGET_LOGIT + 0.5 * jax.random.normal(
        next(ks), (DEPTH, n_pool, PAGE_SIZE, N_HEADS), jnp.float32))
    state_ssm_re = nrm((DEPTH, DEC_BATCH, G, P), 0.1)
    state_ssm_im = nrm((DEPTH, DEC_BATCH, G, P), 0.1)
    page_table = jax.random.permutation(next(ks), n_pool)[:n_used].reshape(DEC_BATCH, n_pages).astype(jnp.int32)
    return {
        'x_prompt': x_prompt,
        'x_sample': x_sample,
        'p_prompt': p_prompt,
        'p_sample': p_sample,
        'cache_k': cache_k,
        'cache_v': cache_v,
        'cache_logf': cache_logf,
        'state_ssm_re': state_ssm_re,
        'state_ssm_im': state_ssm_im,
        'page_table': page_table,
        'g_mix': gain((DEPTH, D_MODEL)),
        'w_in': nrm((DEPTH, D_MODEL, PROJ_WIDTH), D_MODEL ** -0.5),
        'b_f': jnp.linspace(1.0, 5.0, N_HEADS, dtype=jnp.float32)[None, :] + nrm((DEPTH, N_HEADS), 0.1),
        'g_q': gain((DEPTH, N_HEADS, HEAD_DIM)),
        'g_k': gain((DEPTH, N_HEADS, HEAD_DIM)),
        'lam_re': -0.5 + nrm((DEPTH, G, P), 0.01),
        'lam_im': math.pi * jnp.arange(P, dtype=jnp.float32) + nrm((DEPTH, G, P), 0.01),
        'log_dt': jax.random.uniform(next(ks), (DEPTH, G), jnp.float32, math.log(1e-3), math.log(1e-1)),
        'b_re': nrm((DEPTH, G, P, SSM_GROUP), (2 * SSM_GROUP) ** -0.5),
        'b_im': nrm((DEPTH, G, P, SSM_GROUP), (2 * SSM_GROUP) ** -0.5),
        'c_re': nrm((DEPTH, G, SSM_GROUP, P), (2 * P) ** -0.5),
        'c_im': nrm((DEPTH, G, SSM_GROUP, P), (2 * P) ** -0.5),
        'd_skip': nrm((DEPTH, G, SSM_GROUP), 1.0),
        'w_glu': nrm((DEPTH, SSM_WIDTH, SSM_WIDTH), SSM_WIDTH ** -0.5),
        'b_glu': nrm((DEPTH, SSM_WIDTH), 0.01),
        'g_attn_out': gain((DEPTH, ATTN_WIDTH)),
        'g_ssm_out': gain((DEPTH, SSM_WIDTH)),
        'w_out': nrm((DEPTH, ATTN_WIDTH + SSM_WIDTH, D_MODEL), (ATTN_WIDTH + SSM_WIDTH) ** -0.5),
        'g_ffn': gain((DEPTH, D_MODEL)),
        'w_ff1': nrm((DEPTH, D_MODEL, D_FF), D_MODEL ** -0.5),
        'w_ff3': nrm((DEPTH, D_MODEL, D_FF), D_MODEL ** -0.5),
        'w_ff2': nrm((DEPTH, D_FF, D_MODEL), D_FF ** -0.5),
        'w_pe': nrm((DEPTH, PLE_DIM, D_MODEL), PLE_DIM ** -0.5),
        'g_ple': gain((DEPTH, D_MODEL)),
        'w_pg': nrm((DEPTH, D_MODEL, D_MODEL), D_MODEL ** -0.5),
        'b_pg': nrm((DEPTH, D_MODEL), 0.01),
    }


def reference(x_prompt, x_sample, p_prompt, p_sample, cache_k, cache_v, cache_logf,
              state_ssm_re, state_ssm_im, page_table, g_mix, w_in, b_f, g_q, g_k,
              lam_re, lam_im, log_dt, b_re, b_im, c_re, c_im, d_skip, w_glu, b_glu,
              g_attn_out, g_ssm_out, w_out, g_ffn, w_ff1, w_ff3, w_ff2, w_pe, g_ple, w_pg, b_pg):
    f32 = jnp.float32
    dec_b, n_pages = page_table.shape
    past_len = n_pages * cache_k.shape[2]
    h_p, h_s = x_prompt, x_sample
    nk_p, nv_p, nlf_p, nre_p, nim_p = [], [], [], [], []
    nk_s, nv_s, nlf_s, nre_s, nim_s = [], [], [], [], []
    for l in range(DEPTH):
        lw = dict(g_mix=g_mix[l], w_in=w_in[l], b_f=b_f[l], g_q=g_q[l], g_k=g_k[l],
                  lam_re=lam_re[l], lam_im=lam_im[l], log_dt=log_dt[l], b_re=b_re[l], b_im=b_im[l],
                  c_re=c_re[l], c_im=c_im[l], d_skip=d_skip[l], w_glu=w_glu[l], b_glu=b_glu[l],
                  g_attn_out=g_attn_out[l], g_ssm_out=g_ssm_out[l], w_out=w_out[l], g_ffn=g_ffn[l],
                  w_ff1=w_ff1[l], w_ff3=w_ff3[l], w_ff2=w_ff2[l], w_pe=w_pe[l], g_ple=g_ple[l],
                  w_pg=w_pg[l], b_pg=b_pg[l])
        h_p, k_p, v_p, lf_p, st_p = hybrid_layer(h_p, p_prompt[l], fox_attend_prompt, None, lw)
        nk_p.append(k_p); nv_p.append(v_p); nlf_p.append(lf_p)
        nre_p.append(st_p.real); nim_p.append(st_p.imag)
        k_past = cache_k[l][page_table].reshape(dec_b, past_len, N_HEADS, HEAD_DIM)
        v_past = cache_v[l][page_table].reshape(dec_b, past_len, N_HEADS, HEAD_DIM)
        lf_past = cache_logf[l][page_table].reshape(dec_b, past_len, N_HEADS)
        attend_s = functools.partial(fox_attend_sample, k_past=k_past, v_past=v_past, logf_past=lf_past)
        h0 = lax.complex(state_ssm_re[l].astype(f32), state_ssm_im[l].astype(f32))
        h_s, k_s, v_s, lf_s, st_s = hybrid_layer(h_s, p_sample[l], attend_s, h0, lw)
        nk_s.append(k_s); nv_s.append(v_s); nlf_s.append(lf_s)
        nre_s.append(st_s.real); nim_s.append(st_s.imag)
    return (h_p, h_s,
            jnp.stack(nk_p), jnp.stack(nv_p), jnp.stack(nlf_p), jnp.stack(nre_p), jnp.stack(nim_p),
            jnp.stack(nk_s), jnp.stack(nv_s), jnp.stack(nlf_s), jnp.stack(nre_s), jnp.stack(nim_s))
```

```python
import functools
import math

import numpy as np
import jax
import jax.numpy as jnp
from jax import lax
from jax.experimental import pallas as pl
from jax.experimental.pallas import tpu as pltpu

F32 = jnp.float32
BF16 = jnp.bfloat16

RMS_EPS = 1e-6
NEG_INF = -1e30
HEAD_DIM = 64
SSM_GROUP = 16
SSM_STATE = 64
LANES = 128
MXU_DIM = 256
VMEM_LIMIT_BYTES = 56 * 1024 * 1024

PROMPT_CHUNK = 16
ROW_TILE = 512
ATTN_TILE = 512
DEC_PAGES_PER_STEP = 16
SSM_GROUPS_PER_STEP = 4
FF_CHUNK = 256


def _params(semantics):
    return pltpu.CompilerParams(dimension_semantics=semantics, vmem_limit_bytes=VMEM_LIMIT_BYTES)


def _rms(x, g):
    return x * lax.rsqrt(jnp.mean(x * x, axis=-1, keepdims=True) + RMS_EPS) * g


def _sigmoid(x):
    return 1.0 / (1.0 + jnp.exp(-x))


def _log_sigmoid(x):
    return jnp.minimum(x, 0.0) - jnp.log1p(jnp.exp(-jnp.abs(x)))


def _gelu_tanh(x):
    c = math.sqrt(2.0 / math.pi)
    return 0.5 * x * (1.0 + jnp.tanh(c * (x + 0.044715 * (x * x * x))))


def _split3(x):
    h1 = x.astype(BF16)
    r1 = x - h1.astype(F32)
    h2 = r1.astype(BF16)
    h3 = (r1 - h2.astype(F32)).astype(BF16)
    return h1, h2, h3


def _dot(a, b):
    return jnp.dot(a, b, preferred_element_type=F32)


def _dot_nt(a, b):
    return lax.dot_general(a, b, (((1,), (1,)), ((), ())), preferred_element_type=F32)


def _dot3(x, w):
    h1, h2, h3 = _split3(x)
    return _dot(h1, w) + _dot(h2, w) + _dot(h3, w)


def _const_spec(shape):
    nd = len(shape)
    return pl.BlockSpec(shape, lambda *_: (0,) * nd)


def _in_proj_kernel(x_ref, g_ref, wqk_ref, wv_ref, wu_ref, wf_ref, bf_ref, gqk_ref, bd_ref,
                    q_ref, k_ref, v_ref, kb_ref, vb_ref, u_ref, lf_ref, lft_ref, *, n_heads, aw):
    n = _rms(x_ref[0], g_ref[...]).astype(BF16)
    qk = _dot(n, wqk_ref[...])
    bd = bd_ref[...]
    parts = []
    for c in range(2 * aw // MXU_DIM):
        blk = qk[:, c * MXU_DIM:(c + 1) * MXU_DIM]
        ssq = _dot((blk * blk).astype(BF16), bd)
        parts.append(blk * lax.rsqrt(ssq * (1.0 / HEAD_DIM) + RMS_EPS))
    qkn = jnp.concatenate(parts, axis=1) * gqk_ref[...]
    q_ref[0] = (qkn[:, :aw] * (HEAD_DIM ** -0.5)).astype(BF16)
    k = qkn[:, aw:]
    k_ref[0] = k
    kb_ref[0] = k.astype(BF16)
    v = _dot(n, wv_ref[...])
    v_ref[0] = v
    vb_ref[0] = v.astype(BF16)
    u_ref[0] = _dot(n, wu_ref[...]).astype(BF16)
    lf = _log_sigmoid(_dot(n, wf_ref[...]) + bf_ref[...])
    lf_ref[0] = lf[:, :n_heads]
    lft_ref[0] = lf.T[:n_heads, :]


def _in_proj(x, g_mix, wqk, wv, wu, wf, bf, gqk, bd, *, n_heads, tm):
    nb, seqlen, d = x.shape
    aw = wv.shape[1]
    sw = wu.shape[1]
    tm = min(tm, seqlen)
    grid = (nb, seqlen // tm)
    row = lambda w: pl.BlockSpec((1, tm, w), lambda b, i: (b, i, 0))
    out_shape = (
        jax.ShapeDtypeStruct((nb, seqlen, aw), BF16),
        jax.ShapeDtypeStruct((nb, seqlen, aw), F32),
        jax.ShapeDtypeStruct((nb, seqlen, aw), F32),
        jax.ShapeDtypeStruct((nb, seqlen, aw), BF16),
        jax.ShapeDtypeStruct((nb, seqlen, aw), BF16),
        jax.ShapeDtypeStruct((nb, seqlen, sw), BF16),
        jax.ShapeDtypeStruct((nb, seqlen, n_heads), F32),
        jax.ShapeDtypeStruct((nb, n_heads, seqlen), F32),
    )
    out_specs = (row(aw), row(aw), row(aw), row(aw), row(aw), row(sw),
                 pl.BlockSpec((1, tm, n_heads), lambda b, i: (b, i, 0)),
                 pl.BlockSpec((1, n_heads, tm), lambda b, i: (b, 0, i)))
    in_specs = [row(d), _const_spec(g_mix.shape), _const_spec(wqk.shape), _const_spec(wv.shape),
                _const_spec(wu.shape), _const_spec(wf.shape), _const_spec(bf.shape),
                _const_spec(gqk.shape), _const_spec(bd.shape)]
    return pl.pallas_call(
        functools.partial(_in_proj_kernel, n_heads=n_heads, aw=aw),
        grid=grid, in_specs=in_specs, out_specs=out_specs, out_shape=out_shape,
        compiler_params=_params(("parallel", "parallel")), name="in_proj",
    )(x, g_mix, wqk, wv, wu, wf, bf, gqk, bd)


def _negcumsum_kernel(x_ref, tri_ref, ones_ref, o_ref):
    rows, seqlen = x_ref.shape
    tri = tri_ref[...]
    ones = ones_ref[...]
    carry = jnp.zeros((rows, LANES), F32)
    for j in range(seqlen // LANES):
        sl = slice(j * LANES, (j + 1) * LANES)
        xc = x_ref[:, sl]
        o_ref[:, sl] = -(_dot3(xc, tri) + carry)
        carry = carry + _dot3(xc, ones)


def _negcumsum(x):
    tri = jnp.asarray(np.triu(np.ones((LANES, LANES), np.float32)), BF16)
    ones = jnp.ones((LANES, LANES), BF16)
    return pl.pallas_call(
        _negcumsum_kernel, grid=(1,),
        in_specs=[_const_spec(x.shape), _const_spec(tri.shape), _const_spec(ones.shape)],
        out_specs=_const_spec(x.shape), out_shape=jax.ShapeDtypeStruct(x.shape, F32),
        compiler_params=_params(("arbitrary",)), name="negcumsum",
    )(x, tri, ones)


def _attn_kernel(qi_ref, ki_ref, q_ref, k_ref, v_ref, nc_ref, o_ref, m_scr, l_scr, acc_scr,
                 *, n_heads, tq, tk):
    step = pl.program_id(1)
    qi = qi_ref[step]
    ki = ki_ref[step]
    hpg = MXU_DIM // HEAD_DIM
    lane = lax.broadcasted_iota(jnp.int32, (1, MXU_DIM), 1)

    @pl.when(ki == 0)
    def _():
        m_scr[...] = jnp.full(m_scr.shape, NEG_INF, F32)
        l_scr[...] = jnp.zeros(l_scr.shape, F32)
        acc_scr[...] = jnp.zeros(acc_scr.shape, F32)

    def block(masked):
        if masked:
            rowi = lax.broadcasted_iota(jnp.int32, (tq, tk), 0)
            coli = lax.broadcasted_iota(jnp.int32, (tq, tk), 1)
            causal = coli <= rowi
        for h in range(n_heads):
            gsl = slice((h // hpg) * MXU_DIM, (h // hpg + 1) * MXU_DIM)
            hmask = (lane // HEAD_DIM) == (h % hpg)
            qh = jnp.where(hmask, q_ref[0, :, gsl], jnp.zeros((), BF16))
            s = _dot_nt(qh, k_ref[0, :, gsl]) + nc_ref[0, h:h + 1, :]
            if masked:
                s = jnp.where(causal, s, NEG_INF)
            m_prev = m_scr[h]
            m_new = jnp.maximum(m_prev, jnp.max(s, axis=1, keepdims=True))
            alpha = jnp.exp(m_prev - m_new)
            p = jnp.exp(s - jnp.tile(m_new, (1, tk // LANES)))
            l_scr[h] = alpha * l_scr[h] + jnp.sum(p, axis=1, keepdims=True)
            m_scr[h] = m_new
            pv = _dot(p.astype(BF16), v_ref[0, :, gsl])
            acc_scr[h] = acc_scr[h] * jnp.tile(alpha, (1, MXU_DIM // LANES)) + pv

    @pl.when(ki < qi)
    def _():
        block(False)

    @pl.when(ki == qi)
    def _():
        block(True)
        for g in range(n_heads // hpg):
            out = jnp.zeros((tq, MXU_DIM), F32)
            for hh in range(hpg):
                h = g * hpg + hh
                inv = 1.0 / l_scr[h]
                o_h = acc_scr[h] * jnp.tile(inv, (1, MXU_DIM // LANES))
                out = jnp.where((lane // HEAD_DIM) == hh, o_h, out)
            o_ref[0, :, g * MXU_DIM:(g + 1) * MXU_DIM] = out.astype(o_ref.dtype)


def _attn_prompt(q, kb, vb, negc, *, n_heads, tile):
    nb, seqlen, aw = q.shape
    t = min(tile, seqlen)
    nq = seqlen // t
    pairs = [(i, j) for i in range(nq) for j in range(i + 1)]
    qi_tab = jnp.asarray([p[0] for p in pairs], jnp.int32)
    ki_tab = jnp.asarray([p[1] for p in pairs], jnp.int32)
    grid_spec = pltpu.PrefetchScalarGridSpec(
        num_scalar_prefetch=2, grid=(nb, len(pairs)),
        in_specs=[
            pl.BlockSpec((1, t, aw), lambda b, s, qi, ki: (b, qi[s], 0)),
            pl.BlockSpec((1, t, aw), lambda b, s, qi, ki: (b, ki[s], 0)),
            pl.BlockSpec((1, t, aw), lambda b, s, qi, ki: (b, ki[s], 0)),
            pl.BlockSpec((1, n_heads, t), lambda b, s, qi, ki: (b, 0, ki[s])),
        ],
        out_specs=pl.BlockSpec((1, t, aw), lambda b, s, qi, ki: (b, qi[s], 0)),
        scratch_shapes=[pltpu.VMEM((n_heads, t, LANES), F32), pltpu.VMEM((n_heads, t, LANES), F32),
                        pltpu.VMEM((n_heads, t, MXU_DIM), F32)],
    )
    return pl.pallas_call(
        functools.partial(_attn_kernel, n_heads=n_heads, tq=t, tk=t),
        grid_spec=grid_spec, out_shape=jax.ShapeDtypeStruct((nb, seqlen, aw), BF16),
        compiler_params=_params(("parallel", "arbitrary")), name="attn",
    )(qi_tab, ki_tab, q, kb, vb, negc)


def _dec_attn_kernel(pt_ref, q_ref, kn_ref, vn_ref, lfn_ref, tri_ref, ones_ref, *refs,
                     pp, n_heads, n_new):
    k_refs = refs[:pp]
    v_refs = refs[pp:2 * pp]
    lf_refs = refs[2 * pp:3 * pp]
    o_ref = refs[3 * pp]
    qbd_scr, m_scr, l_scr, acc_scr, carry_scr = refs[3 * pp + 1:]
    j = pl.program_id(1)
    rows = n_heads * n_new
    aw = n_heads * HEAD_DIM
    page = k_refs[0].shape[0]

    def expand(r):
        return jnp.broadcast_to(r[:, None, :], (n_heads, n_new, r.shape[-1])).reshape(rows, r.shape[-1])

    @pl.when(j == 0)
    def _():
        qt = jnp.concatenate([q_ref[0]] * n_heads, axis=0)
        rowi = lax.broadcasted_iota(jnp.int32, (rows, aw), 0)
        lani = lax.broadcasted_iota(jnp.int32, (rows, aw), 1)
        qbd = jnp.where((lani // HEAD_DIM) == (rowi // n_new), qt, jnp.zeros((), BF16))
        qbd_scr[...] = qbd
        lfn = lfn_ref[0]
        run = jnp.zeros((1, LANES), F32)
        crow = []
        for t in range(n_new):
            run = run + lfn[t:t + 1, :]
            crow.append(run)
        cnew = jnp.concatenate(crow + [jnp.zeros((LANES - n_new, LANES), F32)], axis=0)
        bias = expand(-(cnew.T[:n_heads, :]))
        s = _dot_nt(qbd, kn_ref[0].astype(BF16)) + bias
        rown = lax.broadcasted_iota(jnp.int32, (rows, LANES), 0)
        coln = lax.broadcasted_iota(jnp.int32, (rows, LANES), 1)
        s = jnp.where(coln <= (rown % n_new), s, NEG_INF)
        m = jnp.max(s, axis=1, keepdims=True)
        p = jnp.exp(s - m)
        m_scr[...] = jnp.broadcast_to(m, m_scr.shape)
        l_scr[...] = jnp.broadcast_to(jnp.sum(p, axis=1, keepdims=True), l_scr.shape)
        acc_scr[...] = _dot(p.astype(BF16), vn_ref[0].astype(BF16))
        carry_scr[...] = jnp.zeros(carry_scr.shape, F32)

    x = jnp.concatenate([lf_refs[i][...] for i in range(pp)], axis=0)
    within = _dot3(x, tri_ref[...])
    tot = _dot3(x, ones_ref[...])
    carry = carry_scr[...]
    s_parts = []
    for i in range(pp):
        sl = slice(i * n_heads, (i + 1) * n_heads)
        bias = expand(within[sl] + carry)
        carry = carry + tot[sl]
        s_parts.append(_dot_nt(qbd_scr[...], k_refs[i][...].astype(BF16)) + bias)
    carry_scr[...] = carry
    s = jnp.concatenate(s_parts, axis=1)
    m_prev = m_scr[...]
    m_new = jnp.maximum(m_prev, jnp.max(s, axis=1, keepdims=True))
    alpha = jnp.exp(m_prev - m_new)
    p = jnp.exp(s - jnp.tile(m_new, (1, pp * page // LANES)))
    l_scr[...] = alpha * l_scr[...] + jnp.sum(p, axis=1, keepdims=True)
    m_scr[...] = m_new
    pv = _dot(p[:, :page].astype(BF16), v_refs[0][...].astype(BF16))
    for i in range(1, pp):
        pv = pv + _dot(p[:, i * page:(i + 1) * page].astype(BF16), v_refs[i][...].astype(BF16))
    acc_scr[...] = acc_scr[...] * jnp.tile(alpha, (1, aw // LANES)) + pv

    @pl.when(j == pl.num_programs(1) - 1)
    def _():
        o = acc_scr[...] * jnp.tile(1.0 / l_scr[...], (1, aw // LANES))
        lano = lax.broadcasted_iota(jnp.int32, (n_new, aw), 1)
        out = jnp.zeros((n_new, aw), F32)
        for h in range(n_heads):
            out = jnp.where((lano // HEAD_DIM) == h, o[h * n_new:(h + 1) * n_new, :], out)
        o_ref[0] = out.astype(o_ref.dtype)


def _attn_sample(q, k_new_pad, v_new_pad, lf_new, cache_k, cache_v, cache_lft, page_table, layer,
                 *, n_heads, pp):
    nb, n_new, aw = q.shape
    n_pages = page_table.shape[1]
    page = cache_k.shape[2]
    pp = min(pp, n_pages)
    nj = n_pages // pp
    rows = n_heads * n_new
    tri = jnp.asarray(np.tril(np.ones((page, page), np.float32), -1), BF16)
    ones = jnp.ones((page, page), BF16)

    def page_spec(i, shape):
        return pl.BlockSpec((None, None) + shape,
                            lambda b, j, pt: (layer, pt[b, n_pages - 1 - (j * pp + i)], 0, 0))

    per_b = lambda shape: pl.BlockSpec((1,) + shape, lambda b, j, pt: (b, 0, 0))
    const = lambda shape: pl.BlockSpec(shape, lambda b, j, pt: (0, 0))
    in_specs = [per_b((n_new, aw)), per_b((page, aw)), per_b((page, aw)), per_b((n_new, LANES)),
                const(tri.shape), const(ones.shape)]
    in_specs += [page_spec(i, (page, aw)) for i in range(pp)]
    in_specs += [page_spec(i, (page, aw)) for i in range(pp)]
    in_specs += [page_spec(i, (n_heads, page)) for i in range(pp)]
    grid_spec = pltpu.PrefetchScalarGridSpec(
        num_scalar_prefetch=1, grid=(nb, nj), in_specs=in_specs,
        out_specs=pl.BlockSpec((1, n_new, aw), lambda b, j, pt: (b, 0, 0)),
        scratch_shapes=[pltpu.VMEM((rows, aw), BF16), pltpu.VMEM((rows, LANES), F32),
                        pltpu.VMEM((rows, LANES), F32), pltpu.VMEM((rows, aw), F32),
                        pltpu.VMEM((n_heads, page), F32)],
    )
    return pl.pallas_call(
        functools.partial(_dec_attn_kernel, pp=pp, n_heads=n_heads, n_new=n_new),
        grid_spec=grid_spec, out_shape=jax.ShapeDtypeStruct((nb, n_new, aw), BF16),
        compiler_params=_params(("parallel", "arbitrary")), name="dec_attn",
    )(page_table, q, k_new_pad, v_new_pad, lf_new, tri, ones,
      *([cache_k] * pp), *([cache_v] * pp), *([cache_lft] * pp))


def _ssm_tables(lam_re, lam_im, log_dt, b_re, b_im, c_re, c_im, chunk):
    g, p = lam_re.shape
    t = chunk
    lam = lax.complex(lam_re.astype(F32), lam_im.astype(F32))
    dt = jnp.exp(log_dt.astype(F32))[:, None]
    lam_dt = lam * dt
    lam_bar = jnp.exp(lam_dt)
    b_bar = ((lam_bar - 1.0) / lam)[:, :, None] * lax.complex(b_re.astype(F32), b_im.astype(F32))
    c = lax.complex(c_re.astype(F32), c_im.astype(F32))
    steps = jnp.arange(t + 1, dtype=F32)
    pw = jnp.exp(lam_dt[None] * steps[:, None, None].astype(jnp.complex64))
    hi = lax.Precision.HIGHEST
    kern = jnp.einsum('ghp,tgp,gpk->gthk', c, pw[:t], b_bar, precision=hi).real
    tt = np.arange(t)
    delta = tt[None, :] - tt[:, None]
    sel = jnp.asarray(np.clip(delta, 0, t - 1))
    mg = kern[:, sel]
    mg = jnp.where(jnp.asarray(delta >= 0)[None, :, :, None, None], mg, 0.0)
    mg = mg.transpose(0, 1, 4, 2, 3).reshape(g, t * SSM_GROUP, t * SSM_GROUP)
    decay = pw[t - 1 - tt]
    bst = decay.transpose(1, 0, 2)[:, :, None, :] * b_bar.transpose(0, 2, 1)[:, None, :, :]
    bg = jnp.concatenate([bst.real, bst.imag], axis=-1).reshape(g, t * SSM_GROUP, 2 * p)
    cl = c[:, None, :, :] * pw[1:].transpose(1, 0, 2)[:, :, None, :]
    cl = cl.transpose(0, 3, 1, 2).reshape(g, p, t * SSM_GROUP)
    cg = jnp.concatenate([cl.real, -cl.imag], axis=1)
    at = pw[t]
    a1 = jnp.concatenate([at.real, at.real], axis=-1).reshape(1, g * 2 * p)
    a2 = jnp.concatenate([-at.imag, at.imag], axis=-1).reshape(1, g * 2 * p)
    return mg.astype(BF16), bg.astype(BF16), cg.astype(BF16), a1, a2


def _ssm_kernel(u_ref, x0_ref, a1_ref, a2_ref, bg_ref, mg_ref, cg_ref, y_ref, xl_ref, s_scr, xin_scr,
                *, n_chunks, nb, gb, th):
    sw = 2 * SSM_STATE
    for g in range(gb):
        s_scr[:, g * sw:(g + 1) * sw] = _dot(u_ref[:, g * th:(g + 1) * th], bg_ref[g])
    a1 = jnp.broadcast_to(a1_ref[...], (nb, gb * sw))
    a2 = jnp.broadcast_to(a2_ref[...], (nb, gb * sw))

    def step(c, x):
        r0 = pl.multiple_of(c * nb, nb)
        xin_scr[pl.ds(r0, nb), :] = x
        swapped = jnp.concatenate(
            [pltpu.roll(x[:, g * sw:(g + 1) * sw], SSM_STATE, 1) for g in range(gb)], axis=1)
        return a1 * x + a2 * swapped + s_scr[pl.ds(r0, nb), :]

    xl_ref[...] = lax.fori_loop(0, n_chunks, step, x0_ref[...])
    for g in range(gb):
        y_ref[:, g * th:(g + 1) * th] = (
            _dot(u_ref[:, g * th:(g + 1) * th], mg_ref[g])
            + _dot(xin_scr[:, g * sw:(g + 1) * sw].astype(BF16), cg_ref[g]))


def _ssm(u_chunked, x0, tables, *, n_chunks, nb, gb):
    mg, bg, cg, a1, a2 = tables
    n_groups, th, _ = mg.shape
    sw = 2 * SSM_STATE
    rows = n_chunks * nb
    gb = min(gb, n_groups)
    col = lambda r, w: pl.BlockSpec((r, gb * w), lambda i: (0, i))
    grp = lambda a, b: pl.BlockSpec((gb, a, b), lambda i: (i, 0, 0))
    return pl.pallas_call(
        functools.partial(_ssm_kernel, n_chunks=n_chunks, nb=nb, gb=gb, th=th),
        grid=(n_groups // gb,),
        in_specs=[col(rows, th), col(nb, sw), col(1, sw), col(1, sw), grp(th, sw), grp(th, th), grp(sw, th)],
        out_specs=(col(rows, th), col(nb, sw)),
        out_shape=(jax.ShapeDtypeStruct((rows, n_groups * th), F32),
                   jax.ShapeDtypeStruct((nb, n_groups * sw), F32)),
        scratch_shapes=[pltpu.VMEM((rows, gb * sw), F32), pltpu.VMEM((rows, gb * sw), F32)],
        compiler_params=_params(("parallel",)), name="ssm",
    )(u_chunked, x0, a1, a2, bg, mg, cg)


def _to_chunks(u, chunk):
    nb, seqlen, w = u.shape
    g = w // SSM_GROUP
    nc = seqlen // chunk
    x = u.reshape(nb, nc, chunk, g, SSM_GROUP).transpose(1, 0, 3, 2, 4)
    return x.reshape(nc * nb, g * chunk * SSM_GROUP)


def _from_chunks(y, nb, chunk):
    rows, w = y.shape
    nc = rows // nb
    g = w // (chunk * SSM_GROUP)
    x = y.reshape(nc, nb, g, chunk, SSM_GROUP).transpose(1, 0, 3, 2, 4)
    return x.reshape(nb, nc * chunk, g * SSM_GROUP)


def _out_ffn_kernel(h_ref, a_ref, y_ref, u_ref, p_ref, dsk_ref, wglu_ref, bglu_ref, ga_ref, gs_ref,
                    woa_ref, wos_ref, gffn_ref, w1_ref, w3_ref, w2_ref, gple_ref, wpg_ref, bpg_ref, wpe_ref,
                    o_ref, *, ff_chunk):
    y = _gelu_tanh(y_ref[...] + dsk_ref[...] * u_ref[...].astype(F32))
    ssm = y * _sigmoid(_dot(y.astype(BF16), wglu_ref[...]) + bglu_ref[...])
    na = _rms(a_ref[...].astype(F32), ga_ref[...]).astype(BF16)
    ns = _rms(ssm, gs_ref[...]).astype(BF16)
    h = h_ref[...] + _dot(na, woa_ref[...]) + _dot(ns, wos_ref[...])
    n2 = _rms(h, gffn_ref[...]).astype(BF16)
    d_ff = w1_ref.shape[1]
    ff = jnp.zeros(h.shape, F32)
    for c in range(d_ff // ff_chunk):
        sl = slice(c * ff_chunk, (c + 1) * ff_chunk)
        a = _dot(n2, w1_ref[:, sl])
        b = _dot(n2, w3_ref[:, sl])
        ff = ff + _dot((a * _sigmoid(a) * b).astype(BF16), w2_ref[sl, :])
    h = h + ff
    n3 = _rms(h, gple_ref[...]).astype(BF16)
    gate = _sigmoid(_dot(n3, wpg_ref[...]) + bpg_ref[...])
    o_ref[...] = h + gate * _dot(p_ref[...].astype(BF16), wpe_ref[...])


def _out_ffn(h, attn, yssm, u, p, lw, *, tm):
    m, d = h.shape
    tm = min(tm, m)
    row = lambda w: pl.BlockSpec((tm, w), lambda i: (i, 0))
    wspec = lambda a: pl.BlockSpec(a.shape, lambda i: (0, 0), pipeline_mode=pl.Buffered(1))
    weights = [lw['d_skip'], lw['w_glu'], lw['b_glu'], lw['g_attn_out'], lw['g_ssm_out'], lw['w_out_a'],
               lw['w_out_s'], lw['g_ffn'], lw['w_ff1'], lw['w_ff3'], lw['w_ff2'], lw['g_ple'], lw['w_pg'],
               lw['b_pg'], lw['w_pe']]
    ff_chunk = FF_CHUNK if lw['w_ff1'].shape[1] % FF_CHUNK == 0 else lw['w_ff1'].shape[1]
    return pl.pallas_call(
        functools.partial(_out_ffn_kernel, ff_chunk=ff_chunk),
        grid=(m // tm,),
        in_specs=[row(d), row(attn.shape[1]), row(yssm.shape[1]), row(u.shape[1]), row(p.shape[1])]
                 + [wspec(w) for w in weights],
        out_specs=row(d), out_shape=jax.ShapeDtypeStruct((m, d), F32),
        compiler_params=_params(("parallel",)), name="out_ffn",
    )(h, attn, yssm, u, p, *weights)


def _layer_weights(l, n_heads, aw, w):
    row = lambda a: a.reshape(1, -1).astype(F32)
    w_in = w['w_in'][l]
    a2, a3 = 2 * aw, 3 * aw
    wf = jnp.zeros((w_in.shape[0], LANES), F32).at[:, :n_heads].set(w_in[:, a3:a3 + n_heads])
    bf = jnp.zeros((1, LANES), F32).at[0, :n_heads].set(w['b_f'][l].astype(F32))
    head_of_lane = np.arange(MXU_DIM) // HEAD_DIM
    bd = jnp.asarray(head_of_lane[:, None] == head_of_lane[None, :], BF16)
    w_out = w['w_out'][l]
    return dict(
        g_mix=row(w['g_mix'][l]), w_qk=w_in[:, :a2].astype(BF16), w_v=w_in[:, a2:a3].astype(BF16),
        w_u=w_in[:, a3 + n_heads:].astype(BF16), w_f=wf.astype(BF16), b_f=bf,
        g_qk=jnp.concatenate([row(w['g_q'][l]), row(w['g_k'][l])], axis=1), bd=bd,
        d_skip=row(w['d_skip'][l]), w_glu=w['w_glu'][l].astype(BF16), b_glu=row(w['b_glu'][l]),
        g_attn_out=row(w['g_attn_out'][l]), g_ssm_out=row(w['g_ssm_out'][l]),
        w_out_a=w_out[:aw].astype(BF16), w_out_s=w_out[aw:].astype(BF16), g_ffn=row(w['g_ffn'][l]),
        w_ff1=w['w_ff1'][l].astype(BF16), w_ff3=w['w_ff3'][l].astype(BF16), w_ff2=w['w_ff2'][l].astype(BF16),
        g_ple=row(w['g_ple'][l]), w_pg=w['w_pg'][l].astype(BF16), b_pg=row(w['b_pg'][l]),
        w_pe=w['w_pe'][l].astype(BF16),
    )


def _split_state(x, n_groups):
    x = x.reshape(x.shape[0], n_groups, 2, SSM_STATE)
    return x[:, :, 0], x[:, :, 1]


def kernel(x_prompt, x_sample, p_prompt, p_sample, cache_k, cache_v, cache_logf, state_ssm_re, state_ssm_im, page_table, g_mix, w_in, b_f, g_q, g_k, lam_re, lam_im, log_dt, b_re, b_im, c_re, c_im, d_skip, w_glu, b_glu, g_attn_out, g_ssm_out, w_out, g_ffn, w_ff1, w_ff3, w_ff2, w_pe, g_ple, w_pg, b_pg):
    w = dict(g_mix=g_mix, w_in=w_in, b_f=b_f, g_q=g_q, g_k=g_k, d_skip=d_skip, w_glu=w_glu, b_glu=b_glu,
             g_attn_out=g_attn_out, g_ssm_out=g_ssm_out, w_out=w_out, g_ffn=g_ffn, w_ff1=w_ff1, w_ff3=w_ff3,
             w_ff2=w_ff2, w_pe=w_pe, g_ple=g_ple, w_pg=w_pg, b_pg=b_pg)
    depth = w_in.shape[0]
    nb, seqlen, d = x_prompt.shape
    db, n_new, _ = x_sample.shape
    n_heads = b_f.shape[1]
    aw = n_heads * HEAD_DIM
    n_groups = log_dt.shape[1]
    n_pool, page = cache_k.shape[1], cache_k.shape[2]
    ck = cache_k.reshape(depth, n_pool, page, aw)
    cv = cache_v.reshape(depth, n_pool, page, aw)
    clft = jnp.swapaxes(cache_logf, 2, 3)
    chunk_p = min(PROMPT_CHUNK, seqlen)

    h_p = x_prompt
    h_s = x_sample.reshape(1, db * n_new, d)
    outs = {k: [] for k in ('k_p', 'v_p', 'lf_p', 're_p', 'im_p', 'k_s', 'v_s', 'lf_s', 're_s', 'im_s')}
    for l in range(depth):
        lw = _layer_weights(l, n_heads, aw, w)
        ssm_par = (lam_re[l], lam_im[l], log_dt[l], b_re[l], b_im[l], c_re[l], c_im[l])
        proj = functools.partial(_in_proj, g_mix=lw['g_mix'], wqk=lw['w_qk'], wv=lw['w_v'], wu=lw['w_u'],
                                 wf=lw['w_f'], bf=lw['b_f'], gqk=lw['g_qk'], bd=lw['bd'], n_heads=n_heads,
                                 tm=ROW_TILE)

        q, k, v, kb, vb, u, lf, lft = proj(h_p)
        negc = _negcumsum(lft.reshape(nb * n_heads, seqlen)).reshape(nb, n_heads, seqlen)
        attn = _attn_prompt(q, kb, vb, negc, n_heads=n_heads, tile=ATTN_TILE)
        y_c, x_last = _ssm(_to_chunks(u, chunk_p), jnp.zeros((nb, n_groups * 2 * SSM_STATE), F32),
                           _ssm_tables(*ssm_par, chunk_p), n_chunks=seqlen // chunk_p, nb=nb,
                           gb=SSM_GROUPS_PER_STEP)
        h_p = _out_ffn(h_p.reshape(nb * seqlen, d), attn.reshape(nb * seqlen, aw),
                       _from_chunks(y_c, nb, chunk_p).reshape(nb * seqlen, -1), u.reshape(nb * seqlen, -1),
                       p_prompt[l].reshape(nb * seqlen, -1), lw, tm=ROW_TILE).reshape(nb, seqlen, d)
        re, im = _split_state(x_last, n_groups)
        outs['k_p'].append(k.reshape(nb, seqlen, n_heads, HEAD_DIM))
        outs['v_p'].append(v.reshape(nb, seqlen, n_heads, HEAD_DIM))
        outs['lf_p'].append(lf)
        outs['re_p'].append(re)
        outs['im_p'].append(im)

        q, k, v, _, _, u, lf, _ = proj(h_s)
        rs = lambda a: a.reshape(db, n_new, a.shape[-1])
        pad_page = lambda a: jnp.pad(rs(a), ((0, 0), (0, page - n_new), (0, 0)))
        lf_pad = jnp.pad(rs(lf), ((0, 0), (0, 0), (0, LANES - n_heads)))
        attn = _attn_sample(rs(q), pad_page(k), pad_page(v), lf_pad, ck, cv, clft, page_table, l,
                            n_heads=n_heads, pp=DEC_PAGES_PER_STEP)
        x0 = jnp.stack([state_ssm_re[l].astype(F32), state_ssm_im[l].astype(F32)], axis=2)
        y_c, x_last = _ssm(_to_chunks(rs(u), n_new), x0.reshape(db, n_groups * 2 * SSM_STATE),
                           _ssm_tables(*ssm_par, n_new), n_chunks=1, nb=db, gb=SSM_GROUPS_PER_STEP)
        h_s = _out_ffn(h_s.reshape(db * n_new, d), attn.reshape(db * n_new, aw),
                       _from_chunks(y_c, db, n_new).reshape(db * n_new, -1), u.reshape(db * n_new, -1),
                       p_sample[l].reshape(db * n_new, -1), lw, tm=ROW_TILE).reshape(1, db * n_new, d)
        re, im = _split_state(x_last, n_groups)
        outs['k_s'].append(k.reshape(db, n_new, n_heads, HEAD_DIM))
        outs['v_s'].append(v.reshape(db, n_new, n_heads, HEAD_DIM))
        outs['lf_s'].append(lf.reshape(db, n_new, n_heads))
        outs['re_s'].append(re)
        outs['im_s'].append(im)

    st = lambda name: jnp.stack(outs[name])
    return (h_p, h_s.reshape(db, n_new, d), st('k_p'), st('v_p'), st('lf_p'), st('re_p'), st('im_p'),
            st('k_s'), st('v_s'), st('lf_s'), st('re_s'), st('im_s'))
```

```python
import functools
import math

import numpy as np
import jax
import jax.numpy as jnp
from jax import lax
from jax.experimental import pallas as pl
from jax.experimental.pallas import tpu as pltpu

F32 = jnp.float32
BF16 = jnp.bfloat16

RMS_EPS = 1e-6
NEG_INF = -1e30
HEAD_DIM = 64
SSM_GROUP = 16
SSM_STATE = 64
LANES = 128
MXU_DIM = 256
VMEM_LIMIT_BYTES = 56 * 1024 * 1024

GROUPS_PER_BLOCK = LANES // SSM_GROUP
STATE_LANES = 2 * SSM_STATE
BLOCK_STATE = GROUPS_PER_BLOCK * STATE_LANES

PROMPT_CHUNK = 16
SSM_CHUNKS_PER_STEP = 32
ROW_TILE = 512
ATTN_TILE = 512
DEC_PAGES_PER_STEP = 16
FF_CHUNK = 256


def _params(semantics):
    return pltpu.CompilerParams(dimension_semantics=semantics, vmem_limit_bytes=VMEM_LIMIT_BYTES)


def _rms(x, g):
    return x * lax.rsqrt(jnp.mean(x * x, axis=-1, keepdims=True) + RMS_EPS) * g


def _sigmoid(x):
    return 1.0 / (1.0 + jnp.exp(-x))


def _log_sigmoid(x):
    return jnp.minimum(x, 0.0) - jnp.log1p(jnp.exp(-jnp.abs(x)))


def _gelu_tanh(x):
    c = math.sqrt(2.0 / math.pi)
    return 0.5 * x * (1.0 + jnp.tanh(c * (x + 0.044715 * (x * x * x))))


def _split3(x):
    h1 = x.astype(BF16)
    r1 = x - h1.astype(F32)
    h2 = r1.astype(BF16)
    h3 = (r1 - h2.astype(F32)).astype(BF16)
    return h1, h2, h3


def _dot(a, b):
    return jnp.dot(a, b, preferred_element_type=F32)


def _dot_nt(a, b):
    return lax.dot_general(a, b, (((1,), (1,)), ((), ())), preferred_element_type=F32)


def _dot3(x, w):
    h1, h2, h3 = _split3(x)
    return _dot(h1, w) + _dot(h2, w) + _dot(h3, w)


def _const_spec(shape):
    nd = len(shape)
    return pl.BlockSpec(shape, lambda *_: (0,) * nd)


def _rows_strided(start, size, stride):
    return pl.ds(start, size) if stride == 1 else pl.ds(start, size, stride=stride)


def _in_proj_kernel(x_ref, g_ref, wqk_ref, wv_ref, wu_ref, wf_ref, bf_ref, gqk_ref, bd_ref,
                    q_ref, k_ref, v_ref, kb_ref, vb_ref, u_ref, lf_ref, lft_ref, *scratch,
                    n_heads, aw, chunk):
    n = _rms(x_ref[0], g_ref[...]).astype(BF16)
    qk = _dot(n, wqk_ref[...])
    bd = bd_ref[...]
    parts = []
    for c in range(2 * aw // MXU_DIM):
        blk = qk[:, c * MXU_DIM:(c + 1) * MXU_DIM]
        ssq = _dot((blk * blk).astype(BF16), bd)
        parts.append(blk * lax.rsqrt(ssq * (1.0 / HEAD_DIM) + RMS_EPS))
    qkn = jnp.concatenate(parts, axis=1) * gqk_ref[...]
    q_ref[0] = (qkn[:, :aw] * (HEAD_DIM ** -0.5)).astype(BF16)
    k = qkn[:, aw:]
    k_ref[0] = k
    kb_ref[0] = k.astype(BF16)
    v = _dot(n, wv_ref[...])
    v_ref[0] = v
    vb_ref[0] = v.astype(BF16)
    u = _dot(n, wu_ref[...])
    if chunk:
        u_scr, = scratch
        rows_c = u.shape[0] // chunk
        for j in range(u.shape[1] // LANES):
            u_scr[j] = u[:, j * LANES:(j + 1) * LANES]
            for t in range(chunk):
                piece = u_scr[j, pl.ds(t, rows_c, stride=chunk), :]
                u_ref[0, :, (j * chunk + t) * LANES:(j * chunk + t + 1) * LANES] = piece.astype(BF16)
    else:
        u_ref[0] = u.astype(BF16)
    lf = _log_sigmoid(_dot(n, wf_ref[...]) + bf_ref[...])
    lf_ref[0] = lf[:, :n_heads]
    lft_ref[0] = lf.T[:n_heads, :]


def _in_proj(x, g_mix, wqk, wv, wu, wf, bf, gqk, bd, *, n_heads, tm, chunk):
    nb, seqlen, d = x.shape
    aw = wv.shape[1]
    sw = wu.shape[1]
    tm = min(tm, seqlen)
    grid = (nb, seqlen // tm)
    row = lambda w: pl.BlockSpec((1, tm, w), lambda b, i: (b, i, 0))
    if chunk:
        u_shape = jax.ShapeDtypeStruct((nb, seqlen // chunk, chunk * sw), BF16)
        u_spec = pl.BlockSpec((1, tm // chunk, chunk * sw), lambda b, i: (b, i, 0))
        scratch = [pltpu.VMEM((sw // LANES, tm, LANES), F32)]
    else:
        u_shape = jax.ShapeDtypeStruct((nb, seqlen, sw), BF16)
        u_spec = row(sw)
        scratch = []
    out_shape = (
        jax.ShapeDtypeStruct((nb, seqlen, aw), BF16),
        jax.ShapeDtypeStruct((nb, seqlen, aw), F32),
        jax.ShapeDtypeStruct((nb, seqlen, aw), F32),
        jax.ShapeDtypeStruct((nb, seqlen, aw), BF16),
        jax.ShapeDtypeStruct((nb, seqlen, aw), BF16),
        u_shape,
        jax.ShapeDtypeStruct((nb, seqlen, n_heads), F32),
        jax.ShapeDtypeStruct((nb, n_heads, seqlen), F32),
    )
    out_specs = (row(aw), row(aw), row(aw), row(aw), row(aw), u_spec,
                 pl.BlockSpec((1, tm, n_heads), lambda b, i: (b, i, 0)),
                 pl.BlockSpec((1, n_heads, tm), lambda b, i: (b, 0, i)))
    in_specs = [row(d), _const_spec(g_mix.shape), _const_spec(wqk.shape), _const_spec(wv.shape),
                _const_spec(wu.shape), _const_spec(wf.shape), _const_spec(bf.shape),
                _const_spec(gqk.shape), _const_spec(bd.shape)]
    return pl.pallas_call(
        functools.partial(_in_proj_kernel, n_heads=n_heads, aw=aw, chunk=chunk),
        grid=grid, in_specs=in_specs, out_specs=out_specs, out_shape=out_shape, scratch_shapes=scratch,
        compiler_params=_params(("parallel", "parallel")), name="in_proj",
    )(x, g_mix, wqk, wv, wu, wf, bf, gqk, bd)


def _negcumsum_kernel(x_ref, tri_ref, ones_ref, o_ref):
    rows, seqlen = x_ref.shape
    tri = tri_ref[...]
    ones = ones_ref[...]
    carry = jnp.zeros((rows, LANES), F32)
    for j in range(seqlen // LANES):
        sl = slice(j * LANES, (j + 1) * LANES)
        xc = x_ref[:, sl]
        o_ref[:, sl] = -(_dot3(xc, tri) + carry)
        carry = carry + _dot3(xc, ones)


def _negcumsum(x):
    tri = jnp.asarray(np.triu(np.ones((LANES, LANES), np.float32)), BF16)
    ones = jnp.ones((LANES, LANES), BF16)
    return pl.pallas_call(
        _negcumsum_kernel, grid=(1,),
        in_specs=[_const_spec(x.shape), _const_spec(tri.shape), _const_spec(ones.shape)],
        out_specs=_const_spec(x.shape), out_shape=jax.ShapeDtypeStruct(x.shape, F32),
        compiler_params=_params(("arbitrary",)), name="negcumsum",
    )(x, tri, ones)


def _attn_kernel(qi_ref, ki_ref, q_ref, k_ref, v_ref, nc_ref, o_ref, m_scr, l_scr, acc_scr,
                 *, n_heads, tq, tk):
    step = pl.program_id(1)
    qi = qi_ref[step]
    ki = ki_ref[step]
    hpg = MXU_DIM // HEAD_DIM
    lane = lax.broadcasted_iota(jnp.int32, (1, MXU_DIM), 1)

    @pl.when(ki == 0)
    def _():
        m_scr[...] = jnp.full(m_scr.shape, NEG_INF, F32)
        l_scr[...] = jnp.zeros(l_scr.shape, F32)
        acc_scr[...] = jnp.zeros(acc_scr.shape, F32)

    def block(masked):
        if masked:
            rowi = lax.broadcasted_iota(jnp.int32, (tq, tk), 0)
            coli = lax.broadcasted_iota(jnp.int32, (tq, tk), 1)
            causal = coli <= rowi
        for h in range(n_heads):
            gsl = slice((h // hpg) * MXU_DIM, (h // hpg + 1) * MXU_DIM)
            hmask = (lane // HEAD_DIM) == (h % hpg)
            qh = jnp.where(hmask, q_ref[0, :, gsl], jnp.zeros((), BF16))
            s = _dot_nt(qh, k_ref[0, :, gsl]) + nc_ref[0, h:h + 1, :]
            if masked:
                s = jnp.where(causal, s, NEG_INF)
            m_prev = m_scr[h]
            m_new = jnp.maximum(m_prev, jnp.max(s, axis=1, keepdims=True))
            alpha = jnp.exp(m_prev - m_new)
            p = jnp.exp(s - jnp.tile(m_new, (1, tk // LANES)))
            l_scr[h] = alpha * l_scr[h] + jnp.sum(p, axis=1, keepdims=True)
            m_scr[h] = m_new
            pv = _dot(p.astype(BF16), v_ref[0, :, gsl])
            acc_scr[h] = acc_scr[h] * jnp.tile(alpha, (1, MXU_DIM // LANES)) + pv

    @pl.when(ki < qi)
    def _():
        block(False)

    @pl.when(ki == qi)
    def _():
        block(True)
        for g in range(n_heads // hpg):
            out = jnp.zeros((tq, MXU_DIM), F32)
            for hh in range(hpg):
                h = g * hpg + hh
                inv = 1.0 / l_scr[h]
                o_h = acc_scr[h] * jnp.tile(inv, (1, MXU_DIM // LANES))
                out = jnp.where((lane // HEAD_DIM) == hh, o_h, out)
            o_ref[0, :, g * MXU_DIM:(g + 1) * MXU_DIM] = out.astype(o_ref.dtype)


def _attn_prompt(q, kb, vb, negc, *, n_heads, tile):
    nb, seqlen, aw = q.shape
    t = min(tile, seqlen)
    nq = seqlen // t
    pairs = [(i, j) for i in range(nq) for j in range(i + 1)]
    qi_tab = jnp.asarray([p[0] for p in pairs], jnp.int32)
    ki_tab = jnp.asarray([p[1] for p in pairs], jnp.int32)
    grid_spec = pltpu.PrefetchScalarGridSpec(
        num_scalar_prefetch=2, grid=(nb, len(pairs)),
        in_specs=[
            pl.BlockSpec((1, t, aw), lambda b, s, qi, ki: (b, qi[s], 0)),
            pl.BlockSpec((1, t, aw), lambda b, s, qi, ki: (b, ki[s], 0)),
            pl.BlockSpec((1, t, aw), lambda b, s, qi, ki: (b, ki[s], 0)),
            pl.BlockSpec((1, n_heads, t), lambda b, s, qi, ki: (b, 0, ki[s])),
        ],
        out_specs=pl.BlockSpec((1, t, aw), lambda b, s, qi, ki: (b, qi[s], 0)),
        scratch_shapes=[pltpu.VMEM((n_heads, t, LANES), F32), pltpu.VMEM((n_heads, t, LANES), F32),
                        pltpu.VMEM((n_heads, t, MXU_DIM), F32)],
    )
    return pl.pallas_call(
        functools.partial(_attn_kernel, n_heads=n_heads, tq=t, tk=t),
        grid_spec=grid_spec, out_shape=jax.ShapeDtypeStruct((nb, seqlen, aw), BF16),
        compiler_params=_params(("parallel", "arbitrary")), name="attn",
    )(qi_tab, ki_tab, q, kb, vb, negc)


def _dec_attn_kernel(pt_ref, q_ref, kn_ref, vn_ref, lfn_ref, tri_ref, ones_ref, *refs,
                     pp, n_heads, n_new):
    k_refs = refs[:pp]
    v_refs = refs[pp:2 * pp]
    lf_refs = refs[2 * pp:3 * pp]
    o_ref = refs[3 * pp]
    qbd_scr, m_scr, l_scr, acc_scr, carry_scr = refs[3 * pp + 1:]
    j = pl.program_id(1)
    rows = n_heads * n_new
    aw = n_heads * HEAD_DIM
    page = k_refs[0].shape[-1]

    def expand(r):
        return jnp.broadcast_to(r[:, None, :], (n_heads, n_new, r.shape[-1])).reshape(rows, r.shape[-1])

    def page_t(ref):
        return ref[...].reshape(aw, page).astype(BF16)

    @pl.when(j == 0)
    def _():
        qt = jnp.concatenate([q_ref[0].astype(F32)] * n_heads, axis=0)
        rowi = lax.broadcasted_iota(jnp.int32, (rows, aw), 0)
        lani = lax.broadcasted_iota(jnp.int32, (rows, aw), 1)
        qbd = jnp.where((lani // HEAD_DIM) == (rowi // n_new), qt, 0.0)
        qbd_scr[...] = qbd.astype(BF16)
        lfn = lfn_ref[0]
        run = jnp.zeros((1, LANES), F32)
        crow = []
        for t in range(n_new):
            run = run + lfn[t:t + 1, :]
            crow.append(run)
        cnew = jnp.concatenate(crow + [jnp.zeros((LANES - n_new, LANES), F32)], axis=0)
        bias = expand(-(cnew.T[:n_heads, :]))
        s = _dot_nt(qbd_scr[...], kn_ref[0].astype(BF16)) + bias
        rown = lax.broadcasted_iota(jnp.int32, (rows, LANES), 0)
        coln = lax.broadcasted_iota(jnp.int32, (rows, LANES), 1)
        s = jnp.where(coln <= (rown % n_new), s, NEG_INF)
        m = jnp.max(s, axis=1, keepdims=True)
        p = jnp.exp(s - m)
        m_scr[...] = jnp.broadcast_to(m, m_scr.shape)
        l_scr[...] = jnp.broadcast_to(jnp.sum(p, axis=1, keepdims=True), l_scr.shape)
        acc_scr[...] = _dot(p.astype(BF16), vn_ref[0].astype(BF16))
        carry_scr[...] = jnp.zeros(carry_scr.shape, F32)

    x = jnp.concatenate([lf_refs[i][...] for i in range(pp)], axis=0)
    within = _dot3(x, tri_ref[...])
    tot = _dot3(x, ones_ref[...])
    carry = carry_scr[...]
    qbd = qbd_scr[...]
    s_parts = []
    for i in range(pp):
        sl = slice(i * n_heads, (i + 1) * n_heads)
        bias = expand(within[sl] + carry)
        carry = carry + tot[sl]
        s_parts.append(_dot(qbd, page_t(k_refs[i])) + bias)
    carry_scr[...] = carry
    s = jnp.concatenate(s_parts, axis=1)
    m_prev = m_scr[...]
    m_new = jnp.maximum(m_prev, jnp.max(s, axis=1, keepdims=True))
    alpha = jnp.exp(m_prev - m_new)
    p = jnp.exp(s - jnp.tile(m_new, (1, pp * page // LANES)))
    l_scr[...] = alpha * l_scr[...] + jnp.sum(p, axis=1, keepdims=True)
    m_scr[...] = m_new
    pv = _dot_nt(p[:, :page].astype(BF16), page_t(v_refs[0]))
    for i in range(1, pp):
        pv = pv + _dot_nt(p[:, i * page:(i + 1) * page].astype(BF16), page_t(v_refs[i]))
    acc_scr[...] = acc_scr[...] * jnp.tile(alpha, (1, aw // LANES)) + pv

    @pl.when(j == pl.num_programs(1) - 1)
    def _():
        o = acc_scr[...] * jnp.tile(1.0 / l_scr[...], (1, aw // LANES))
        lano = lax.broadcasted_iota(jnp.int32, (n_new, aw), 1)
        out = jnp.zeros((n_new, aw), F32)
        for h in range(n_heads):
            out = jnp.where((lano // HEAD_DIM) == h, o[h * n_new:(h + 1) * n_new, :], out)
        o_ref[0] = out.astype(o_ref.dtype)


def _attn_sample(q, k_new_pad, v_new_pad, lf_new, cache_k, cache_v, cache_lft, page_table, layer,
                 *, n_heads, pp):
    nb, n_new, aw = q.shape
    n_pages = page_table.shape[1]
    page = cache_k.shape[-1]
    pp = min(pp, n_pages)
    nj = n_pages // pp
    rows = n_heads * n_new
    tri = jnp.asarray(np.tril(np.ones((page, page), np.float32), -1), BF16)
    ones = jnp.ones((page, page), BF16)

    def page_spec(i, shape):
        zeros = (0,) * len(shape)
        return pl.BlockSpec((None, None) + shape,
                            lambda b, j, pt: (layer, pt[b, n_pages - 1 - (j * pp + i)]) + zeros)

    per_b = lambda shape: pl.BlockSpec((1,) + shape, lambda b, j, pt: (b, 0, 0))
    const = lambda shape: pl.BlockSpec(shape, lambda b, j, pt: (0, 0))
    in_specs = [per_b((n_new, aw)), per_b((page, aw)), per_b((page, aw)), per_b((n_new, LANES)),
                const(tri.shape), const(ones.shape)]
    in_specs += [page_spec(i, (n_heads, HEAD_DIM, page)) for i in range(pp)]
    in_specs += [page_spec(i, (n_heads, HEAD_DIM, page)) for i in range(pp)]
    in_specs += [page_spec(i, (n_heads, page)) for i in range(pp)]
    grid_spec = pltpu.PrefetchScalarGridSpec(
        num_scalar_prefetch=1, grid=(nb, nj), in_specs=in_specs,
        out_specs=pl.BlockSpec((1, n_new, aw), lambda b, j, pt: (b, 0, 0)),
        scratch_shapes=[pltpu.VMEM((rows, aw), BF16), pltpu.VMEM((rows, LANES), F32),
                        pltpu.VMEM((rows, LANES), F32), pltpu.VMEM((rows, aw), F32),
                        pltpu.VMEM((n_heads, page), F32)],
    )
    return pl.pallas_call(
        functools.partial(_dec_attn_kernel, pp=pp, n_heads=n_heads, n_new=n_new),
        grid_spec=grid_spec, out_shape=jax.ShapeDtypeStruct((nb, n_new, aw), BF16),
        compiler_params=_params(("parallel", "arbitrary")), name="dec_attn",
    )(page_table, q, k_new_pad, v_new_pad, lf_new, tri, ones,
      *([cache_k] * pp), *([cache_v] * pp), *([cache_lft] * pp))


def _ssm_tables(lam_re, lam_im, log_dt, b_re, b_im, c_re, c_im, d_skip, chunk):
    g, p = lam_re.shape
    t = chunk
    nt = t // 2
    gpb = GROUPS_PER_BLOCK
    nj = g // gpb
    lam = lax.complex(lam_re.astype(F32), lam_im.astype(F32))
    dt = jnp.exp(log_dt.astype(F32))[:, None]
    lam_dt = lam * dt
    lam_bar = jnp.exp(lam_dt)
    b_bar = ((lam_bar - 1.0) / lam)[:, :, None] * lax.complex(b_re.astype(F32), b_im.astype(F32))
    c = lax.complex(c_re.astype(F32), c_im.astype(F32))
    steps = jnp.arange(t + 1, dtype=F32)
    pw = jnp.exp(lam_dt[None] * steps[:, None, None].astype(jnp.complex64))
    hi = lax.Precision.HIGHEST
    eye = jnp.eye(gpb, dtype=F32)
    kern = jnp.einsum('ghp,tgp,gpk->gthk', c, pw[:t], b_bar, precision=hi).real
    kern = kern.reshape(nj, gpb, t, SSM_GROUP, SSM_GROUP)
    dd, s2, t2 = np.meshgrid(np.arange(nt), np.arange(2), np.arange(2), indexing='ij')
    tau = 2 * dd + t2 - s2
    m = kern[:, :, jnp.asarray(np.clip(tau, 0, t - 1))]
    m = jnp.where(jnp.asarray(tau >= 0)[None, None, :, :, :, None, None], m, 0.0)
    mt = jnp.einsum('jkdstab,kl->jdskbtla', m, eye).reshape(nj, nt, MXU_DIM, MXU_DIM)
    tt = np.arange(t)
    decay = pw[t - 1 - tt]
    bst = decay.transpose(1, 0, 2)[:, :, None, :] * b_bar.transpose(0, 2, 1)[:, None, :, :]
    bst = jnp.stack([bst.real, bst.imag], axis=3).reshape(nj, gpb, nt, 2, SSM_GROUP, 2, p)
    bt = jnp.einsum('jkbshrp,kl->jbskhlrp', bst, eye).reshape(nj, nt, MXU_DIM, BLOCK_STATE)
    cl = c[:, None, :, :] * pw[1:].transpose(1, 0, 2)[:, :, None, :]
    cc = jnp.stack([cl.real, -cl.imag], axis=3).reshape(nj, gpb, nt, 2, SSM_GROUP, 2, p)
    ct = jnp.einsum('jkathrp,kl->jalrptkh', cc, eye).reshape(nj, nt, BLOCK_STATE, MXU_DIM)
    at = pw[t]
    a1 = jnp.concatenate([at.real, at.real], axis=-1).reshape(1, g * STATE_LANES)
    a2 = jnp.concatenate([-at.imag, at.imag], axis=-1).reshape(1, g * STATE_LANES)
    dsk = jnp.broadcast_to(d_skip.astype(F32).reshape(nj, 1, gpb * SSM_GROUP), (nj, t, LANES)).reshape(1, nj * t * LANES)
    return mt.astype(BF16), bt.astype(BF16), ct.astype(BF16), a1, a2, dsk


def _ssm_kernel(u_ref, x0_ref, a1_ref, a2_ref, dsk_ref, bt_ref, mt_ref, ct_ref, y_ref, xl_ref,
                s_scr, xin_scr, x_scr, *, nb, cb, nt):
    rows = nb * cb
    u = u_ref[...].reshape(rows, nt * MXU_DIM)
    tile = lambda a: slice(a * MXU_DIM, (a + 1) * MXU_DIM)

    @pl.when(pl.program_id(1) == 0)
    def _():
        x_scr[...] = x0_ref[...]

    s = _dot(u[:, tile(0)], bt_ref[0, 0])
    for b in range(1, nt):
        s = s + _dot(u[:, tile(b)], bt_ref[0, b])
    gsl = lambda g: slice(g * STATE_LANES, (g + 1) * STATE_LANES)
    for g in range(GROUPS_PER_BLOCK):
        s_scr[g] = s[:, gsl(g)]
    a1 = jnp.broadcast_to(a1_ref[...], (nb, BLOCK_STATE))
    a2 = jnp.broadcast_to(a2_ref[...], (nb, BLOCK_STATE))

    def step(c, x):
        sel = _rows_strided(c, nb, cb)
        for g in range(GROUPS_PER_BLOCK):
            xin_scr[g, sel, :] = x[:, gsl(g)]
        swapped = jnp.concatenate(
            [pltpu.roll(x[:, gsl(g)], SSM_STATE, 1) for g in range(GROUPS_PER_BLOCK)], axis=1)
        s_c = jnp.concatenate([s_scr[g, sel, :] for g in range(GROUPS_PER_BLOCK)], axis=1)
        return a1 * x + a2 * swapped + s_c

    x = lax.fori_loop(0, cb, step, x_scr[...])
    x_scr[...] = x
    xl_ref[...] = x
    xin = jnp.concatenate([xin_scr[g] for g in range(GROUPS_PER_BLOCK)], axis=1).astype(BF16)
    for a in range(nt):
        acc = _dot(xin, ct_ref[0, a])
        for b in range(a + 1):
            acc = acc + _dot(u[:, tile(b)], mt_ref[0, a - b])
        acc = acc + dsk_ref[:, tile(a)] * u[:, tile(a)].astype(F32)
        if len(y_ref.shape) == 3:
            y_ref[:, :, tile(a)] = acc.reshape(nb, cb, MXU_DIM)
        else:
            y_ref[:, tile(a)] = acc


def _ssm(u_c, x0, tables, *, cb):
    mt, bt, ct, a1, a2, dsk = tables
    nj, nt = mt.shape[0], mt.shape[1]
    w = nt * MXU_DIM
    nb = u_c.shape[0]
    if u_c.ndim == 3:
        n_chunks = u_c.shape[1]
        cb = min(cb, n_chunks)
        io_spec = pl.BlockSpec((nb, cb, w), lambda j, r: (0, r, j))
    else:
        n_chunks = cb = 1
        io_spec = pl.BlockSpec((nb, w), lambda j, r: (0, j))
    per_j = lambda r, width: pl.BlockSpec((r, width), lambda j, r_: (0, j))
    tab = lambda a, b: pl.BlockSpec((1, nt, a, b), lambda j, r: (j, 0, 0, 0))
    rows = nb * cb
    return pl.pallas_call(
        functools.partial(_ssm_kernel, nb=nb, cb=cb, nt=nt),
        grid=(nj, n_chunks // cb),
        in_specs=[io_spec, per_j(nb, BLOCK_STATE), per_j(1, BLOCK_STATE), per_j(1, BLOCK_STATE), per_j(1, w),
                  tab(MXU_DIM, BLOCK_STATE), tab(MXU_DIM, MXU_DIM), tab(BLOCK_STATE, MXU_DIM)],
        out_specs=(io_spec, per_j(nb, BLOCK_STATE)),
        out_shape=(jax.ShapeDtypeStruct(u_c.shape, F32),
                   jax.ShapeDtypeStruct((nb, nj * BLOCK_STATE), F32)),
        scratch_shapes=[pltpu.VMEM((GROUPS_PER_BLOCK, rows, STATE_LANES), F32),
                        pltpu.VMEM((GROUPS_PER_BLOCK, rows, STATE_LANES), F32),
                        pltpu.VMEM((nb, BLOCK_STATE), F32)],
        compiler_params=_params(("parallel", "arbitrary")), name="ssm",
    )(u_c, x0, a1, a2, dsk, bt, mt, ct)


def _to_chunk_row(u):
    nb, t, w = u.shape
    return u.reshape(nb, t, w // LANES, LANES).transpose(0, 2, 1, 3).reshape(nb, t * w)


def _from_chunk_row(y, t):
    nb, tw = y.shape
    w = tw // t
    return y.reshape(nb, w // LANES, t, LANES).transpose(0, 2, 1, 3).reshape(nb, t, w)


def _out_ffn_kernel(h_ref, a_ref, y_ref, p_ref, wglu_ref, bglu_ref, ga_ref, gs_ref,
                    woa_ref, wos_ref, gffn_ref, w1_ref, w3_ref, w2_ref, gple_ref, wpg_ref, bpg_ref, wpe_ref,
                    o_ref, *scratch, ff_chunk, chunk):
    if chunk:
        y_scr, = scratch
        rows_c = y_ref.shape[0]
        for j in range(y_scr.shape[0]):
            for t in range(chunk):
                y_scr[j, pl.ds(t, rows_c, stride=chunk), :] = (
                    y_ref[:, (j * chunk + t) * LANES:(j * chunk + t + 1) * LANES])
        y = jnp.concatenate([y_scr[j] for j in range(y_scr.shape[0])], axis=1)
    else:
        y = y_ref[...]
    y = _gelu_tanh(y)
    ssm = y * _sigmoid(_dot(y.astype(BF16), wglu_ref[...]) + bglu_ref[...])
    na = _rms(a_ref[...].astype(F32), ga_ref[...]).astype(BF16)
    ns = _rms(ssm, gs_ref[...]).astype(BF16)
    h = h_ref[...] + _dot(na, woa_ref[...]) + _dot(ns, wos_ref[...])
    n2 = _rms(h, gffn_ref[...]).astype(BF16)
    d_ff = w1_ref.shape[1]
    ff = jnp.zeros(h.shape, F32)
    for c in range(d_ff // ff_chunk):
        sl = slice(c * ff_chunk, (c + 1) * ff_chunk)
        a = _dot(n2, w1_ref[:, sl])
        b = _dot(n2, w3_ref[:, sl])
        ff = ff + _dot((a * _sigmoid(a) * b).astype(BF16), w2_ref[sl, :])
    h = h + ff
    n3 = _rms(h, gple_ref[...]).astype(BF16)
    gate = _sigmoid(_dot(n3, wpg_ref[...]) + bpg_ref[...])
    o_ref[...] = h + gate * _dot(p_ref[...].astype(BF16), wpe_ref[...])


def _out_ffn(h, attn, yssm, p, lw, *, tm, chunk):
    m, d = h.shape
    tm = min(tm, m)
    sw = lw['w_glu'].shape[0]
    row = lambda w: pl.BlockSpec((tm, w), lambda i: (i, 0))
    wspec = lambda a: pl.BlockSpec(a.shape, lambda i: (0, 0), pipeline_mode=pl.Buffered(1))
    weights = [lw['w_glu'], lw['b_glu'], lw['g_attn_out'], lw['g_ssm_out'], lw['w_out_a'],
               lw['w_out_s'], lw['g_ffn'], lw['w_ff1'], lw['w_ff3'], lw['w_ff2'], lw['g_ple'], lw['w_pg'],
               lw['b_pg'], lw['w_pe']]
    ff_chunk = FF_CHUNK if lw['w_ff1'].shape[1] % FF_CHUNK == 0 else lw['w_ff1'].shape[1]
    if chunk:
        y_spec = pl.BlockSpec((tm // chunk, chunk * sw), lambda i: (i, 0))
        scratch = [pltpu.VMEM((sw // LANES, tm, LANES), F32)]
    else:
        y_spec = row(sw)
        scratch = []
    return pl.pallas_call(
        functools.partial(_out_ffn_kernel, ff_chunk=ff_chunk, chunk=chunk),
        grid=(m // tm,),
        in_specs=[row(d), row(attn.shape[1]), y_spec, row(p.shape[1])] + [wspec(w) for w in weights],
        out_specs=row(d), out_shape=jax.ShapeDtypeStruct((m, d), F32), scratch_shapes=scratch,
        compiler_params=_params(("parallel",)), name="out_ffn",
    )(h, attn, yssm, p, *weights)


def _layer_weights(l, n_heads, aw, w):
    row = lambda a: a.reshape(1, -1).astype(F32)
    w_in = w['w_in'][l]
    a2, a3 = 2 * aw, 3 * aw
    wf = jnp.zeros((w_in.shape[0], LANES), F32).at[:, :n_heads].set(w_in[:, a3:a3 + n_heads])
    bf = jnp.zeros((1, LANES), F32).at[0, :n_heads].set(w['b_f'][l].astype(F32))
    head_of_lane = np.arange(MXU_DIM) // HEAD_DIM
    bd = jnp.asarray(head_of_lane[:, None] == head_of_lane[None, :], BF16)
    w_out = w['w_out'][l]
    return dict(
        g_mix=row(w['g_mix'][l]), w_qk=w_in[:, :a2].astype(BF16), w_v=w_in[:, a2:a3].astype(BF16),
        w_u=w_in[:, a3 + n_heads:].astype(BF16), w_f=wf.astype(BF16), b_f=bf,
        g_qk=jnp.concatenate([row(w['g_q'][l]), row(w['g_k'][l])], axis=1), bd=bd,
        w_glu=w['w_glu'][l].astype(BF16), b_glu=row(w['b_glu'][l]),
        g_attn_out=row(w['g_attn_out'][l]), g_ssm_out=row(w['g_ssm_out'][l]),
        w_out_a=w_out[:aw].astype(BF16), w_out_s=w_out[aw:].astype(BF16), g_ffn=row(w['g_ffn'][l]),
        w_ff1=w['w_ff1'][l].astype(BF16), w_ff3=w['w_ff3'][l].astype(BF16), w_ff2=w['w_ff2'][l].astype(BF16),
        g_ple=row(w['g_ple'][l]), w_pg=w['w_pg'][l].astype(BF16), b_pg=row(w['b_pg'][l]),
        w_pe=w['w_pe'][l].astype(BF16),
    )


def _split_state(x, n_groups):
    x = x.reshape(x.shape[0], n_groups, 2, SSM_STATE)
    return x[:, :, 0], x[:, :, 1]


def kernel(x_prompt, x_sample, p_prompt, p_sample, cache_k, cache_v, cache_logf, state_ssm_re, state_ssm_im, page_table, g_mix, w_in, b_f, g_q, g_k, lam_re, lam_im, log_dt, b_re, b_im, c_re, c_im, d_skip, w_glu, b_glu, g_attn_out, g_ssm_out, w_out, g_ffn, w_ff1, w_ff3, w_ff2, w_pe, g_ple, w_pg, b_pg):
    w = dict(g_mix=g_mix, w_in=w_in, b_f=b_f, g_q=g_q, g_k=g_k, w_glu=w_glu, b_glu=b_glu,
             g_attn_out=g_attn_out, g_ssm_out=g_ssm_out, w_out=w_out, g_ffn=g_ffn, w_ff1=w_ff1, w_ff3=w_ff3,
             w_ff2=w_ff2, w_pe=w_pe, g_ple=g_ple, w_pg=w_pg, b_pg=b_pg)
    depth = w_in.shape[0]
    nb, seqlen, d = x_prompt.shape
    db, n_new, _ = x_sample.shape
    n_heads = b_f.shape[1]
    aw = n_heads * HEAD_DIM
    n_groups = log_dt.shape[1]
    n_pool, page = cache_k.shape[1], cache_k.shape[2]
    ck = jnp.transpose(cache_k, (0, 1, 3, 4, 2))
    cv = jnp.transpose(cache_v, (0, 1, 3, 4, 2))
    clft = jnp.swapaxes(cache_logf, 2, 3)
    chunk_p = min(PROMPT_CHUNK, seqlen)

    h_p = x_prompt
    h_s = x_sample.reshape(1, db * n_new, d)
    outs = {k: [] for k in ('k_p', 'v_p', 'lf_p', 're_p', 'im_p', 'k_s', 'v_s', 'lf_s', 're_s', 'im_s')}
    for l in range(depth):
        lw = _layer_weights(l, n_heads, aw, w)
        ssm_par = (lam_re[l], lam_im[l], log_dt[l], b_re[l], b_im[l], c_re[l], c_im[l], d_skip[l])
        proj = functools.partial(_in_proj, g_mix=lw['g_mix'], wqk=lw['w_qk'], wv=lw['w_v'], wu=lw['w_u'],
                                 wf=lw['w_f'], bf=lw['b_f'], gqk=lw['g_qk'], bd=lw['bd'], n_heads=n_heads,
                                 tm=ROW_TILE)

        q, k, v, kb, vb, u_c, lf, lft = proj(h_p, chunk=chunk_p)
        negc = _negcumsum(lft.reshape(nb * n_heads, seqlen)).reshape(nb, n_heads, seqlen)
        attn = _attn_prompt(q, kb, vb, negc, n_heads=n_heads, tile=ATTN_TILE)
        y_c, x_last = _ssm(u_c, jnp.zeros((nb, n_groups * STATE_LANES), F32), _ssm_tables(*ssm_par, chunk_p),
                           cb=SSM_CHUNKS_PER_STEP)
        h_p = _out_ffn(h_p.reshape(nb * seqlen, d), attn.reshape(nb * seqlen, aw),
                       y_c.reshape(nb * seqlen // chunk_p, -1), p_prompt[l].reshape(nb * seqlen, -1), lw,
                       tm=ROW_TILE, chunk=chunk_p).reshape(nb, seqlen, d)
        re, im = _split_state(x_last, n_groups)
        outs['k_p'].append(k.reshape(nb, seqlen, n_heads, HEAD_DIM))
        outs['v_p'].append(v.reshape(nb, seqlen, n_heads, HEAD_DIM))
        outs['lf_p'].append(jnp.swapaxes(lft, 1, 2))
        outs['re_p'].append(re)
        outs['im_p'].append(im)

        q, k, v, _, _, u, lf, _ = proj(h_s, chunk=0)
        rs = lambda a: a.reshape(db, n_new, a.shape[-1])
        pad_page = lambda a: jnp.pad(rs(a), ((0, 0), (0, page - n_new), (0, 0)))
        lf_pad = jnp.pad(rs(lf), ((0, 0), (0, 0), (0, LANES - n_heads)))
        attn = _attn_sample(rs(q), pad_page(k), pad_page(v), lf_pad, ck, cv, clft, page_table, l,
                            n_heads=n_heads, pp=DEC_PAGES_PER_STEP)
        x0 = jnp.stack([state_ssm_re[l].astype(F32), state_ssm_im[l].astype(F32)], axis=2)
        y_c, x_last = _ssm(_to_chunk_row(rs(u)), x0.reshape(db, n_groups * STATE_LANES),
                           _ssm_tables(*ssm_par, n_new), cb=1)
        h_s = _out_ffn(h_s.reshape(db * n_new, d), attn.reshape(db * n_new, aw),
                       _from_chunk_row(y_c, n_new).reshape(db * n_new, -1), p_sample[l].reshape(db * n_new, -1),
                       lw, tm=ROW_TILE, chunk=0).reshape(1, db * n_new, d)
        re, im = _split_state(x_last, n_groups)
        outs['k_s'].append(k.reshape(db, n_new, n_heads, HEAD_DIM))
        outs['v_s'].append(v.reshape(db, n_new, n_heads, HEAD_DIM))
        outs['lf_s'].append(lf.reshape(db, n_new, n_heads))
        outs['re_s'].append(re)
        outs['im_s'].append(im)

    st = lambda name: jnp.stack(outs[name])
    return (h_p, h_s.reshape(db, n_new, d), st('k_p'), st('v_p'), st('lf_p'), st('re_p'), st('im_p'),
            st('k_s'), st('v_s'), st('lf_s'), st('re_s'), st('im_s'))
```

```python
import functools
import math

import numpy as np
import jax
import jax.numpy as jnp
from jax import lax
from jax.experimental import pallas as pl
from jax.experimental.pallas import tpu as pltpu

F32 = jnp.float32
BF16 = jnp.bfloat16

RMS_EPS = 1e-6
NEG_INF = -1e30
LOG2E = math.log2(math.e)
HEAD_DIM = 64
SSM_GROUP = 16
SSM_STATE = 64
LANES = 128
MXU_DIM = 256
VMEM_LIMIT_BYTES = 56 * 1024 * 1024

GROUPS_PER_BLOCK = LANES // SSM_GROUP
HALF_STATE = GROUPS_PER_BLOCK * SSM_STATE
BLOCK_STATE = 2 * HALF_STATE

PROMPT_CHUNK = 16
SSM_CHUNKS_PER_STEP = 32
ROW_TILE = 512
ATTN_TILE = 512
DEC_PAGES_PER_STEP = 16
FF_CHUNK = 256


def _params(semantics):
    return pltpu.CompilerParams(dimension_semantics=semantics, vmem_limit_bytes=VMEM_LIMIT_BYTES)


def _rms(x, g):
    return x * lax.rsqrt(jnp.mean(x * x, axis=-1, keepdims=True) + RMS_EPS) * g


def _sigmoid(x):
    return 1.0 / (1.0 + jnp.exp(-x))


def _log_sigmoid(x):
    return jnp.minimum(x, 0.0) - jnp.log1p(jnp.exp(-jnp.abs(x)))


def _gelu_tanh(x):
    c = math.sqrt(2.0 / math.pi)
    return 0.5 * x * (1.0 + jnp.tanh(c * (x + 0.044715 * (x * x * x))))


def _split3(x):
    h1 = x.astype(BF16)
    r1 = x - h1.astype(F32)
    h2 = r1.astype(BF16)
    h3 = (r1 - h2.astype(F32)).astype(BF16)
    return h1, h2, h3


def _dot(a, b):
    return jnp.dot(a, b, preferred_element_type=F32)


def _dot_nt(a, b):
    return lax.dot_general(a, b, (((1,), (1,)), ((), ())), preferred_element_type=F32)


def _dot3(x, w):
    h1, h2, h3 = _split3(x)
    return _dot(h1, w) + _dot(h2, w) + _dot(h3, w)


def _dot_nt_split(a, b):
    a1, a2, _ = _split3(a)
    b1, b2, _ = _split3(b)
    return _dot_nt(a1, b1) + _dot_nt(a1, b2) + _dot_nt(a2, b1)


def _const_spec(shape):
    nd = len(shape)
    return pl.BlockSpec(shape, lambda *_: (0,) * nd)


def _layer_spec(arr, layer, **kw):
    zeros = (0,) * (arr.ndim - 1)
    return pl.BlockSpec((None,) + arr.shape[1:], lambda *_: (layer,) + zeros, **kw)


def _rows_strided(start, size, stride):
    return pl.ds(start, size) if stride == 1 else pl.ds(start, size, stride=stride)


def _in_proj_kernel(x_ref, g_ref, w_ref, bf_ref, gqk_ref, bd_ref,
                    q_ref, k_ref, v_ref, kb_ref, vb_ref, u_ref, lf_ref, lft_ref, *scratch,
                    n_heads, aw, sw, chunk):
    n = _rms(x_ref[0], g_ref[...]).astype(BF16)
    proj = _dot(n, w_ref[...])
    bd = bd_ref[...]
    parts = []
    for c in range(2 * aw // MXU_DIM):
        blk = proj[:, c * MXU_DIM:(c + 1) * MXU_DIM]
        ssq = _dot((blk * blk).astype(BF16), bd)
        parts.append(blk * lax.rsqrt(ssq * (1.0 / HEAD_DIM) + RMS_EPS))
    qkn = jnp.concatenate(parts, axis=1) * gqk_ref[...]
    q_ref[0] = (qkn[:, :aw] * (LOG2E * HEAD_DIM ** -0.5)).astype(BF16)
    k = qkn[:, aw:]
    k_ref[0] = k
    kb_ref[0] = k.astype(BF16)
    v = proj[:, 2 * aw:3 * aw]
    v_ref[0] = v
    vb_ref[0] = v.astype(BF16)
    u = proj[:, 3 * aw:3 * aw + sw]
    if chunk:
        u_scr, = scratch
        rows_c = u.shape[0] // chunk
        for j in range(sw // LANES):
            u_scr[j] = u[:, j * LANES:(j + 1) * LANES]
            for t in range(chunk):
                piece = u_scr[j, pl.ds(t, rows_c, stride=chunk), :]
                u_ref[0, :, (j * chunk + t) * LANES:(j * chunk + t + 1) * LANES] = piece.astype(BF16)
    else:
        u_ref[0] = u.astype(BF16)
    lf = _log_sigmoid(proj[:, 3 * aw + sw:] + bf_ref[...])
    lf_ref[0] = lf[:, :n_heads]
    lft_ref[0] = lf.T[:n_heads, :]


def _in_proj(x, pw, layer, *, n_heads, tm, chunk):
    nb, seqlen, d = x.shape
    aw = n_heads * HEAD_DIM
    sw = pw['w_in'].shape[2] - 3 * aw - LANES
    tm = min(tm, seqlen)
    grid = (nb, seqlen // tm)
    row = lambda w: pl.BlockSpec((1, tm, w), lambda b, i: (b, i, 0))
    if chunk:
        u_shape = jax.ShapeDtypeStruct((nb, seqlen // chunk, chunk * sw), BF16)
        u_spec = pl.BlockSpec((1, tm // chunk, chunk * sw), lambda b, i: (b, i, 0))
        scratch = [pltpu.VMEM((sw // LANES, tm, LANES), F32)]
    else:
        u_shape = jax.ShapeDtypeStruct((nb, seqlen, sw), BF16)
        u_spec = row(sw)
        scratch = []
    out_shape = (
        jax.ShapeDtypeStruct((nb, seqlen, aw), BF16),
        jax.ShapeDtypeStruct((nb, seqlen, aw), F32),
        jax.ShapeDtypeStruct((nb, seqlen, aw), F32),
        jax.ShapeDtypeStruct((nb, seqlen, aw), BF16),
        jax.ShapeDtypeStruct((nb, seqlen, aw), BF16),
        u_shape,
        jax.ShapeDtypeStruct((nb, seqlen, n_heads), F32),
        jax.ShapeDtypeStruct((nb, n_heads, seqlen), F32),
    )
    out_specs = (row(aw), row(aw), row(aw), row(aw), row(aw), u_spec,
                 pl.BlockSpec((1, tm, n_heads), lambda b, i: (b, i, 0)),
                 pl.BlockSpec((1, n_heads, tm), lambda b, i: (b, 0, i)))
    weights = [pw['g_mix'], pw['w_in'], pw['b_f'], pw['g_qk']]
    in_specs = [row(d)] + [_layer_spec(a, layer) for a in weights] + [_const_spec(pw['bd'].shape)]
    return pl.pallas_call(
        functools.partial(_in_proj_kernel, n_heads=n_heads, aw=aw, sw=sw, chunk=chunk),
        grid=grid, in_specs=in_specs, out_specs=out_specs, out_shape=out_shape, scratch_shapes=scratch,
        compiler_params=_params(("parallel", "parallel")), name="in_proj",
    )(x, *weights, pw['bd'])


def _negcumsum_kernel(x_ref, tri_ref, ones_ref, o_ref):
    rows, seqlen = x_ref.shape
    tri = tri_ref[...]
    ones = ones_ref[...]
    carry = jnp.zeros((rows, LANES), F32)
    for j in range(seqlen // LANES):
        sl = slice(j * LANES, (j + 1) * LANES)
        xc = x_ref[:, sl]
        o_ref[:, sl] = -LOG2E * (_dot3(xc, tri) + carry)
        carry = carry + _dot3(xc, ones)


def _negcumsum(x):
    tri = jnp.asarray(np.triu(np.ones((LANES, LANES), np.float32)), BF16)
    ones = jnp.ones((LANES, LANES), BF16)
    return pl.pallas_call(
        _negcumsum_kernel, grid=(1,),
        in_specs=[_const_spec(x.shape), _const_spec(tri.shape), _const_spec(ones.shape)],
        out_specs=_const_spec(x.shape), out_shape=jax.ShapeDtypeStruct(x.shape, F32),
        compiler_params=_params(("arbitrary",)), name="negcumsum",
    )(x, tri, ones)


def _attn_kernel(qi_ref, ki_ref, q_ref, k_ref, v_ref, nc_ref, o_ref, m_scr, acc_scr,
                 *, n_heads, tq, tk):
    step = pl.program_id(1)
    qi = qi_ref[step]
    ki = ki_ref[step]
    lane = lax.broadcasted_iota(jnp.int32, (1, LANES), 1)
    zero = jnp.zeros((), BF16)

    def own_lanes(h):
        return (lane < HEAD_DIM) if h % 2 == 0 else (lane >= HEAD_DIM)

    def denom_lane(h):
        return HEAD_DIM if h % 2 == 0 else 0

    @pl.when(ki == 0)
    def _():
        m_scr[...] = jnp.full(m_scr.shape, NEG_INF, F32)
        acc_scr[...] = jnp.zeros(acc_scr.shape, F32)

    def block(masked):
        if masked:
            rowi = lax.broadcasted_iota(jnp.int32, (tq, tk), 0)
            coli = lax.broadcasted_iota(jnp.int32, (tq, tk), 1)
            causal = coli <= rowi
        for h in range(n_heads):
            psl = slice((h // 2) * LANES, (h // 2 + 1) * LANES)
            own = own_lanes(h)
            qh = jnp.where(own, q_ref[0, :, psl], zero)
            s = _dot_nt(qh, k_ref[0, :, psl]) + nc_ref[0, h:h + 1, :]
            if masked:
                s = jnp.where(causal, s, NEG_INF)
            m_prev = m_scr[h]
            m_new = jnp.maximum(m_prev, jnp.max(s, axis=1, keepdims=True))
            p = jnp.exp2(s - jnp.tile(m_new, (1, tk // LANES)))
            m_scr[h] = m_new
            vh = jnp.where(own, v_ref[0, :, psl], jnp.where(lane == denom_lane(h), 1.0, 0.0).astype(BF16))
            acc_scr[h] = acc_scr[h] * jnp.exp2(m_prev - m_new) + _dot(p.astype(BF16), vh)

    @pl.when(ki < qi)
    def _():
        block(False)

    @pl.when(ki == qi)
    def _():
        block(True)
        for pair in range(n_heads // 2):
            halves = []
            for h in (2 * pair, 2 * pair + 1):
                acc = acc_scr[h]
                halves.append(acc * (1.0 / acc[:, denom_lane(h):denom_lane(h) + 1]))
            out = jnp.where(own_lanes(0), halves[0], halves[1])
            o_ref[0, :, pair * LANES:(pair + 1) * LANES] = out.astype(o_ref.dtype)


def _attn_prompt(q, kb, vb, negc, *, n_heads, tile):
    nb, seqlen, aw = q.shape
    t = min(tile, seqlen)
    nq = seqlen // t
    pairs = [(i, j) for i in range(nq) for j in range(i + 1)]
    qi_tab = jnp.asarray([p[0] for p in pairs], jnp.int32)
    ki_tab = jnp.asarray([p[1] for p in pairs], jnp.int32)
    grid_spec = pltpu.PrefetchScalarGridSpec(
        num_scalar_prefetch=2, grid=(nb, len(pairs)),
        in_specs=[
            pl.BlockSpec((1, t, aw), lambda b, s, qi, ki: (b, qi[s], 0)),
            pl.BlockSpec((1, t, aw), lambda b, s, qi, ki: (b, ki[s], 0)),
            pl.BlockSpec((1, t, aw), lambda b, s, qi, ki: (b, ki[s], 0)),
            pl.BlockSpec((1, n_heads, t), lambda b, s, qi, ki: (b, 0, ki[s])),
        ],
        out_specs=pl.BlockSpec((1, t, aw), lambda b, s, qi, ki: (b, qi[s], 0)),
        scratch_shapes=[pltpu.VMEM((n_heads, t, LANES), F32), pltpu.VMEM((n_heads, t, LANES), F32)],
    )
    return pl.pallas_call(
        functools.partial(_attn_kernel, n_heads=n_heads, tq=t, tk=t),
        grid_spec=grid_spec, out_shape=jax.ShapeDtypeStruct((nb, seqlen, aw), BF16),
        compiler_params=_params(("parallel", "arbitrary")), name="attn",
    )(qi_tab, ki_tab, q, kb, vb, negc)


def _dec_attn_kernel(pt_ref, q_ref, kn_ref, vn_ref, lfn_ref, tri_ref, ones_ref, *refs,
                     pp, n_heads, n_new):
    k_refs = refs[:pp]
    v_refs = refs[pp:2 * pp]
    lf_refs = refs[2 * pp:3 * pp]
    o_ref = refs[3 * pp]
    qbd_scr, m_scr, l_scr, acc_scr, carry_scr = refs[3 * pp + 1:]
    j = pl.program_id(1)
    rows = n_heads * n_new
    aw = n_heads * HEAD_DIM
    page = k_refs[0].shape[-1]

    def expand(r):
        return jnp.broadcast_to(r[:, None, :], (n_heads, n_new, r.shape[-1])).reshape(rows, r.shape[-1])

    def page_t(ref):
        return ref[...].reshape(aw, page).astype(BF16)

    @pl.when(j == 0)
    def _():
        qt = jnp.concatenate([q_ref[0].astype(F32)] * n_heads, axis=0)
        rowi = lax.broadcasted_iota(jnp.int32, (rows, aw), 0)
        lani = lax.broadcasted_iota(jnp.int32, (rows, aw), 1)
        qbd = jnp.where((lani // HEAD_DIM) == (rowi // n_new), qt, 0.0)
        qbd_scr[...] = qbd.astype(BF16)
        lfn = lfn_ref[0]
        run = jnp.zeros((1, LANES), F32)
        crow = []
        for t in range(n_new):
            run = run + lfn[t:t + 1, :]
            crow.append(run)
        cnew = jnp.concatenate(crow + [jnp.zeros((LANES - n_new, LANES), F32)], axis=0)
        bias = expand(-LOG2E * cnew.T[:n_heads, :])
        s = _dot_nt(qbd_scr[...], kn_ref[0].astype(BF16)) + bias
        rown = lax.broadcasted_iota(jnp.int32, (rows, LANES), 0)
        coln = lax.broadcasted_iota(jnp.int32, (rows, LANES), 1)
        s = jnp.where(coln <= (rown % n_new), s, NEG_INF)
        m = jnp.max(s, axis=1, keepdims=True)
        p = jnp.exp2(s - m)
        m_scr[...] = jnp.broadcast_to(m, m_scr.shape)
        l_scr[...] = jnp.broadcast_to(jnp.sum(p, axis=1, keepdims=True), l_scr.shape)
        acc_scr[...] = _dot(p.astype(BF16), vn_ref[0].astype(BF16))
        carry_scr[...] = jnp.zeros(carry_scr.shape, F32)

    x = jnp.concatenate([lf_refs[i][...] for i in range(pp)], axis=0)
    within = _dot3(x, tri_ref[...])
    tot = _dot3(x, ones_ref[...])
    carry = carry_scr[...]
    qbd = qbd_scr[...]
    s_parts = []
    for i in range(pp):
        sl = slice(i * n_heads, (i + 1) * n_heads)
        bias = expand(LOG2E * (within[sl] + carry))
        carry = carry + tot[sl]
        s_parts.append(_dot(qbd, page_t(k_refs[i])) + bias)
    carry_scr[...] = carry
    s = jnp.concatenate(s_parts, axis=1)
    m_prev = m_scr[...]
    m_new = jnp.maximum(m_prev, jnp.max(s, axis=1, keepdims=True))
    alpha = jnp.exp2(m_prev - m_new)
    p = jnp.exp2(s - jnp.tile(m_new, (1, pp * page // LANES)))
    l_scr[...] = alpha * l_scr[...] + jnp.sum(p, axis=1, keepdims=True)
    m_scr[...] = m_new
    pv = _dot_nt(p[:, :page].astype(BF16), page_t(v_refs[0]))
    for i in range(1, pp):
        pv = pv + _dot_nt(p[:, i * page:(i + 1) * page].astype(BF16), page_t(v_refs[i]))
    acc_scr[...] = acc_scr[...] * jnp.tile(alpha, (1, aw // LANES)) + pv

    @pl.when(j == pl.num_programs(1) - 1)
    def _():
        o = acc_scr[...] * jnp.tile(1.0 / l_scr[...], (1, aw // LANES))
        lano = lax.broadcasted_iota(jnp.int32, (n_new, aw), 1)
        out = jnp.zeros((n_new, aw), F32)
        for h in range(n_heads):
            out = jnp.where((lano // HEAD_DIM) == h, o[h * n_new:(h + 1) * n_new, :], out)
        o_ref[0] = out.astype(o_ref.dtype)


def _attn_sample(q, k_new_pad, v_new_pad, lf_new, cache_k, cache_v, cache_lft, page_table, layer,
                 *, n_heads, pp):
    nb, n_new, aw = q.shape
    n_pages = page_table.shape[1]
    page = cache_k.shape[-1]
    pp = min(pp, n_pages)
    nj = n_pages // pp
    rows = n_heads * n_new
    tri = jnp.asarray(np.tril(np.ones((page, page), np.float32), -1), BF16)
    ones = jnp.ones((page, page), BF16)

    def page_spec(i, shape):
        zeros = (0,) * len(shape)
        return pl.BlockSpec((None, None) + shape,
                            lambda b, j, pt: (layer, pt[b, n_pages - 1 - (j * pp + i)]) + zeros)

    per_b = lambda shape: pl.BlockSpec((1,) + shape, lambda b, j, pt: (b, 0, 0))
    const = lambda shape: pl.BlockSpec(shape, lambda b, j, pt: (0, 0))
    in_specs = [per_b((n_new, aw)), per_b((page, aw)), per_b((page, aw)), per_b((n_new, LANES)),
                const(tri.shape), const(ones.shape)]
    in_specs += [page_spec(i, (n_heads, HEAD_DIM, page)) for i in range(pp)]
    in_specs += [page_spec(i, (n_heads, HEAD_DIM, page)) for i in range(pp)]
    in_specs += [page_spec(i, (n_heads, page)) for i in range(pp)]
    grid_spec = pltpu.PrefetchScalarGridSpec(
        num_scalar_prefetch=1, grid=(nb, nj), in_specs=in_specs,
        out_specs=pl.BlockSpec((1, n_new, aw), lambda b, j, pt: (b, 0, 0)),
        scratch_shapes=[pltpu.VMEM((rows, aw), BF16), pltpu.VMEM((rows, LANES), F32),
                        pltpu.VMEM((rows, LANES), F32), pltpu.VMEM((rows, aw), F32),
                        pltpu.VMEM((n_heads, page), F32)],
    )
    return pl.pallas_call(
        functools.partial(_dec_attn_kernel, pp=pp, n_heads=n_heads, n_new=n_new),
        grid_spec=grid_spec, out_shape=jax.ShapeDtypeStruct((nb, n_new, aw), BF16),
        compiler_params=_params(("parallel", "arbitrary")), name="dec_attn",
    )(page_table, q, k_new_pad, v_new_pad, lf_new, tri, ones,
      *([cache_k] * pp), *([cache_v] * pp), *([cache_lft] * pp))


def _ssm_block_params(lam_re, lam_im, log_dt, b_re, b_im, c_re, c_im, d_skip, t_max):
    depth, g, p = lam_re.shape
    gpb = GROUPS_PER_BLOCK
    nj = g // gpb
    lam = lax.complex(lam_re.astype(F32), lam_im.astype(F32))
    lam_dt = lam * jnp.exp(log_dt.astype(F32))[..., None]
    lam_bar = jnp.exp(lam_dt)
    b_bar = ((lam_bar - 1.0) / lam)[..., None] * lax.complex(b_re.astype(F32), b_im.astype(F32))
    c = lax.complex(c_re.astype(F32), c_im.astype(F32))
    steps = jnp.arange(t_max + 1, dtype=F32).astype(jnp.complex64)
    pw = jnp.exp(lam_dt[:, :, None, :] * steps[None, None, :, None])
    pw = pw.reshape(depth, nj, gpb, t_max + 1, p).transpose(0, 1, 3, 2, 4).reshape(depth, nj, t_max + 1, gpb * p)
    eye = jnp.asarray(np.eye(gpb, dtype=bool))[None, None, :, None, :, None]

    def block_diag(a):
        a = jnp.where(eye, a[:, :, :, :, None, :], 0.0)
        return a.reshape(depth, nj, gpb * SSM_GROUP, gpb * p)

    bmat = block_diag(b_bar.transpose(0, 1, 3, 2).reshape(depth, nj, gpb, SSM_GROUP, p))
    cmat = block_diag(c.reshape(depth, nj, gpb, SSM_GROUP, p))
    ri = lambda z: (jnp.real(z), jnp.imag(z))
    dsk = d_skip.astype(F32).reshape(depth, 1, g * SSM_GROUP)
    return ri(bmat), ri(cmat), ri(pw), dsk


def _ssm_tables_kernel(bre_ref, bim_ref, cre_ref, cim_ref, pwr_ref, pwi_ref, mt_ref, bt_ref, ct_ref, *, t):
    bre, bim = bre_ref[...], bim_ref[...]
    cre, cim = cre_ref[...], cim_ref[...]
    half = lambda i: slice(i * LANES, (i + 1) * LANES)
    lo, hi = slice(0, HALF_STATE), slice(HALF_STATE, BLOCK_STATE)
    bd = []
    for e in range(t):
        pr, pi = pwr_ref[e:e + 1, :], pwi_ref[e:e + 1, :]
        xr = bre * pr - bim * pi
        xi = bre * pi + bim * pr
        s = t - 1 - e
        bt_ref[s // 2, half(s % 2), lo] = xr.astype(BF16)
        bt_ref[s // 2, half(s % 2), hi] = xi.astype(BF16)
        bd.append(_dot_nt_split(xr, cre) - _dot_nt_split(xi, cim))
        pr, pi = pwr_ref[e + 1:e + 2, :], pwi_ref[e + 1:e + 2, :]
        ct_ref[e // 2, half(e % 2), lo] = (cre * pr - cim * pi).astype(BF16)
        ct_ref[e // 2, half(e % 2), hi] = (-(cre * pi + cim * pr)).astype(BF16)
    zero = jnp.zeros((LANES, LANES), BF16)
    for d in range(t // 2):
        mt_ref[d, half(0), half(0)] = bd[2 * d].astype(BF16)
        mt_ref[d, half(0), half(1)] = bd[2 * d + 1].astype(BF16)
        mt_ref[d, half(1), half(0)] = bd[2 * d - 1].astype(BF16) if d else zero
        mt_ref[d, half(1), half(1)] = bd[2 * d].astype(BF16)


def _ssm_tables(block_params, t):
    (bre, bim), (cre, cim), (pwr, pwi), _ = block_params
    depth, nj = bre.shape[:2]
    nt = t // 2
    mat = pl.BlockSpec((None, None) + bre.shape[2:], lambda l, j: (l, j, 0, 0))
    pws = pl.BlockSpec((None, None) + pwr.shape[2:], lambda l, j: (l, j, 0, 0))
    tile = lambda a, b: pl.BlockSpec((None, None, nt, a, b), lambda l, j: (l, j, 0, 0, 0))
    shape = lambda a, b: jax.ShapeDtypeStruct((depth, nj, nt, a, b), BF16)
    return pl.pallas_call(
        functools.partial(_ssm_tables_kernel, t=t), grid=(depth, nj),
        in_specs=[mat, mat, mat, mat, pws, pws],
        out_specs=(tile(MXU_DIM, MXU_DIM), tile(MXU_DIM, BLOCK_STATE), tile(MXU_DIM, BLOCK_STATE)),
        out_shape=(shape(MXU_DIM, MXU_DIM), shape(MXU_DIM, BLOCK_STATE), shape(MXU_DIM, BLOCK_STATE)),
        compiler_params=_params(("parallel", "parallel")), name="ssm_tables",
    )(bre, bim, cre, cim, pwr, pwi)


def _ssm_scan_params(block_params, t):
    _, _, (pwr, pwi), dsk = block_params
    depth, nj = pwr.shape[:2]
    ar, ai = pwr[:, :, t], pwi[:, :, t]
    a1 = jnp.concatenate([ar, ar], axis=-1).reshape(depth, 1, nj * BLOCK_STATE)
    a2 = jnp.concatenate([-ai, ai], axis=-1).reshape(depth, 1, nj * BLOCK_STATE)
    dsk_c = jnp.broadcast_to(dsk.reshape(depth, nj, 1, LANES), (depth, nj, t, LANES)).reshape(depth, 1, nj * t * LANES)
    return a1, a2, dsk_c


def _ssm_kernel(u_ref, x0_ref, a1_ref, a2_ref, dsk_ref, bt_ref, mt_ref, ct_ref, y_ref, xl_ref,
                s_scr, xin_scr, x_scr, *, nb, cb, nt):
    rows = nb * cb
    n_lt = BLOCK_STATE // LANES
    u = u_ref[...].reshape(rows, nt * MXU_DIM)
    tile = lambda a: slice(a * MXU_DIM, (a + 1) * MXU_DIM)
    lt = lambda i: slice(i * LANES, (i + 1) * LANES)

    @pl.when(pl.program_id(1) == 0)
    def _():
        x_scr[...] = x0_ref[...]

    s = _dot(u[:, tile(0)], bt_ref[0])
    for b in range(1, nt):
        s = s + _dot(u[:, tile(b)], bt_ref[b])
    for i in range(n_lt):
        s_scr[i] = s[:, lt(i)]
    a1 = jnp.broadcast_to(a1_ref[...], (nb, BLOCK_STATE))
    a2 = jnp.broadcast_to(a2_ref[...], (nb, BLOCK_STATE))

    def step(c, x):
        sel = _rows_strided(c, nb, cb)
        for i in range(n_lt):
            xin_scr[i, sel, :] = x[:, lt(i)]
        swapped = jnp.concatenate([x[:, HALF_STATE:], x[:, :HALF_STATE]], axis=1)
        s_c = jnp.concatenate([s_scr[i, sel, :] for i in range(n_lt)], axis=1)
        return a1 * x + a2 * swapped + s_c

    x = lax.fori_loop(0, cb, step, x_scr[...])
    x_scr[...] = x
    xl_ref[...] = x
    xin = jnp.concatenate([xin_scr[i] for i in range(n_lt)], axis=1).astype(BF16)
    for a in range(nt):
        acc = _dot_nt(xin, ct_ref[a])
        for b in range(a + 1):
            acc = acc + _dot(u[:, tile(b)], mt_ref[a - b])
        acc = acc + dsk_ref[:, tile(a)] * u[:, tile(a)].astype(F32)
        if len(y_ref.shape) == 3:
            y_ref[:, :, tile(a)] = acc.reshape(nb, cb, MXU_DIM)
        else:
            y_ref[:, tile(a)] = acc


def _ssm(u_c, x0, tables, scan_params, layer, *, cb):
    mt, bt, ct = tables
    a1, a2, dsk = scan_params
    nj, nt = mt.shape[1], mt.shape[2]
    w = nt * MXU_DIM
    nb = u_c.shape[0]
    if u_c.ndim == 3:
        n_chunks = u_c.shape[1]
        cb = min(cb, n_chunks)
        io_spec = pl.BlockSpec((nb, cb, w), lambda j, r: (0, r, j))
    else:
        n_chunks = cb = 1
        io_spec = pl.BlockSpec((nb, w), lambda j, r: (0, j))
    state = pl.BlockSpec((nb, BLOCK_STATE), lambda j, r: (0, j))
    vec = lambda width: pl.BlockSpec((None, 1, width), lambda j, r: (layer, 0, j))
    tab = lambda a: pl.BlockSpec((None, None) + a.shape[2:], lambda j, r: (layer, j, 0, 0, 0))
    rows = nb * cb
    n_lt = BLOCK_STATE // LANES
    return pl.pallas_call(
        functools.partial(_ssm_kernel, nb=nb, cb=cb, nt=nt),
        grid=(nj, n_chunks // cb),
        in_specs=[io_spec, state, vec(BLOCK_STATE), vec(BLOCK_STATE), vec(w), tab(bt), tab(mt), tab(ct)],
        out_specs=(io_spec, state),
        out_shape=(jax.ShapeDtypeStruct(u_c.shape, F32),
                   jax.ShapeDtypeStruct((nb, nj * BLOCK_STATE), F32)),
        scratch_shapes=[pltpu.VMEM((n_lt, rows, LANES), F32), pltpu.VMEM((n_lt, rows, LANES), F32),
                        pltpu.VMEM((nb, BLOCK_STATE), F32)],
        compiler_params=_params(("parallel", "arbitrary")), name="ssm",
    )(u_c, x0, a1, a2, dsk, bt, mt, ct)


def _to_chunk_row(u):
    nb, t, w = u.shape
    return u.reshape(nb, t, w // LANES, LANES).transpose(0, 2, 1, 3).reshape(nb, t * w)


def _from_chunk_row(y, t):
    nb, tw = y.shape
    w = tw // t
    return y.reshape(nb, w // LANES, t, LANES).transpose(0, 2, 1, 3).reshape(nb, t, w)


def _pack_state(re, im):
    nb, g, p = re.shape
    nj = g // GROUPS_PER_BLOCK
    x = jnp.stack([re.astype(F32).reshape(nb, nj, HALF_STATE), im.astype(F32).reshape(nb, nj, HALF_STATE)], axis=2)
    return x.reshape(nb, nj * BLOCK_STATE)


def _unpack_state(x, n_groups):
    nb = x.shape[0]
    x = x.reshape(nb, n_groups // GROUPS_PER_BLOCK, 2, GROUPS_PER_BLOCK, SSM_STATE)
    return x[:, :, 0].reshape(nb, n_groups, SSM_STATE), x[:, :, 1].reshape(nb, n_groups, SSM_STATE)


def _out_ffn_kernel(h_ref, a_ref, y_ref, p_ref, wglu_ref, bglu_ref, ga_ref, gs_ref,
                    wo_ref, gffn_ref, w1_ref, w3_ref, w2_ref, gple_ref, wpg_ref, bpg_ref, wpe_ref,
                    o_ref, *scratch, ff_chunk, chunk):
    if chunk:
        y_scr, = scratch
        rows_c = y_ref.shape[0]
        for j in range(y_scr.shape[0]):
            for t in range(chunk):
                y_scr[j, pl.ds(t, rows_c, stride=chunk), :] = (
                    y_ref[:, (j * chunk + t) * LANES:(j * chunk + t + 1) * LANES])
        y = jnp.concatenate([y_scr[j] for j in range(y_scr.shape[0])], axis=1)
    else:
        y = y_ref[...]
    y = _gelu_tanh(y)
    ssm = y * _sigmoid(_dot(y.astype(BF16), wglu_ref[...]) + bglu_ref[...])
    na = _rms(a_ref[...].astype(F32), ga_ref[...]).astype(BF16)
    ns = _rms(ssm, gs_ref[...]).astype(BF16)
    aw = na.shape[1]
    h = h_ref[...] + _dot(na, wo_ref[:aw, :]) + _dot(ns, wo_ref[aw:, :])
    n2 = _rms(h, gffn_ref[...]).astype(BF16)
    d_ff = w1_ref.shape[1]
    ff = jnp.zeros(h.shape, F32)
    for c in range(d_ff // ff_chunk):
        sl = slice(c * ff_chunk, (c + 1) * ff_chunk)
        a = _dot(n2, w1_ref[:, sl])
        b = _dot(n2, w3_ref[:, sl])
        ff = ff + _dot((a * _sigmoid(a) * b).astype(BF16), w2_ref[sl, :])
    h = h + ff
    n3 = _rms(h, gple_ref[...]).astype(BF16)
    gate = _sigmoid(_dot(n3, wpg_ref[...]) + bpg_ref[...])
    o_ref[...] = h + gate * _dot(p_ref[...].astype(BF16), wpe_ref[...])


def _out_ffn(h, attn, yssm, p_all, pw, layer, *, tm, chunk):
    m, d = h.shape
    tm = min(tm, m)
    sw = pw['w_glu'].shape[1]
    row = lambda w: pl.BlockSpec((tm, w), lambda i: (i, 0))
    weights = [pw[k] for k in ('w_glu', 'b_glu', 'g_attn_out', 'g_ssm_out', 'w_out', 'g_ffn', 'w_ff1', 'w_ff3',
                               'w_ff2', 'g_ple', 'w_pg', 'b_pg', 'w_pe')]
    d_ff = pw['w_ff1'].shape[2]
    ff_chunk = FF_CHUNK if d_ff % FF_CHUNK == 0 else d_ff
    if chunk:
        y_spec = pl.BlockSpec((tm // chunk, chunk * sw), lambda i: (i, 0))
        scratch = [pltpu.VMEM((sw // LANES, tm, LANES), F32)]
    else:
        y_spec = row(sw)
        scratch = []
    p_spec = pl.BlockSpec((None, tm, p_all.shape[2]), lambda i: (layer, i, 0))
    return pl.pallas_call(
        functools.partial(_out_ffn_kernel, ff_chunk=ff_chunk, chunk=chunk),
        grid=(m // tm,),
        in_specs=[row(d), row(attn.shape[1]), y_spec, p_spec]
                 + [_layer_spec(a, layer, pipeline_mode=pl.Buffered(1)) for a in weights],
        out_specs=row(d), out_shape=jax.ShapeDtypeStruct((m, d), F32), scratch_shapes=scratch,
        compiler_params=_params(("parallel",)), name="out_ffn",
    )(h, attn, yssm, p_all, *weights)


def _prep_weights(n_heads, aw, w):
    depth = w['w_in'].shape[0]
    row = lambda a: a.reshape(depth, 1, -1).astype(F32)
    w_in = w['w_in']
    a3 = 3 * aw
    f_cols = jnp.pad(w_in[:, :, a3:a3 + n_heads], ((0, 0), (0, 0), (0, LANES - n_heads)))
    w_in_k = jnp.concatenate([w_in[:, :, :a3], w_in[:, :, a3 + n_heads:], f_cols], axis=2).astype(BF16)
    head_of_lane = np.arange(MXU_DIM) // HEAD_DIM
    bf16 = lambda name: w[name].astype(BF16)
    return dict(
        g_mix=row(w['g_mix']), w_in=w_in_k,
        b_f=jnp.pad(row(w['b_f']), ((0, 0), (0, 0), (0, LANES - n_heads))),
        g_qk=jnp.concatenate([row(w['g_q']), row(w['g_k'])], axis=2),
        bd=jnp.asarray(head_of_lane[:, None] == head_of_lane[None, :], BF16),
        w_glu=bf16('w_glu'), b_glu=row(w['b_glu']), g_attn_out=row(w['g_attn_out']), g_ssm_out=row(w['g_ssm_out']),
        w_out=bf16('w_out'), g_ffn=row(w['g_ffn']), w_ff1=bf16('w_ff1'), w_ff3=bf16('w_ff3'), w_ff2=bf16('w_ff2'),
        g_ple=row(w['g_ple']), w_pg=bf16('w_pg'), b_pg=row(w['b_pg']), w_pe=bf16('w_pe'),
    )


def kernel(x_prompt, x_sample, p_prompt, p_sample, cache_k, cache_v, cache_logf, state_ssm_re, state_ssm_im, page_table, g_mix, w_in, b_f, g_q, g_k, lam_re, lam_im, log_dt, b_re, b_im, c_re, c_im, d_skip, w_glu, b_glu, g_attn_out, g_ssm_out, w_out, g_ffn, w_ff1, w_ff3, w_ff2, w_pe, g_ple, w_pg, b_pg):
    w = dict(g_mix=g_mix, w_in=w_in, b_f=b_f, g_q=g_q, g_k=g_k, w_glu=w_glu, b_glu=b_glu,
             g_attn_out=g_attn_out, g_ssm_out=g_ssm_out, w_out=w_out, g_ffn=g_ffn, w_ff1=w_ff1, w_ff3=w_ff3,
             w_ff2=w_ff2, w_pe=w_pe, g_ple=g_ple, w_pg=w_pg, b_pg=b_pg)
    depth = w_in.shape[0]
    nb, seqlen, d = x_prompt.shape
    db, n_new, _ = x_sample.shape
    n_heads = b_f.shape[1]
    aw = n_heads * HEAD_DIM
    n_groups = log_dt.shape[1]
    page = cache_k.shape[2]
    ck = jnp.transpose(cache_k, (0, 1, 3, 4, 2))
    cv = jnp.transpose(cache_v, (0, 1, 3, 4, 2))
    clft = jnp.swapaxes(cache_logf, 2, 3)
    chunk_p = min(PROMPT_CHUNK, seqlen)

    pw = _prep_weights(n_heads, aw, w)
    blocks = _ssm_block_params(lam_re, lam_im, log_dt, b_re, b_im, c_re, c_im, d_skip, max(chunk_p, n_new))
    tables_p, scan_p = _ssm_tables(blocks, chunk_p), _ssm_scan_params(blocks, chunk_p)
    tables_s, scan_s = _ssm_tables(blocks, n_new), _ssm_scan_params(blocks, n_new)
    pp_all = p_prompt.reshape(depth, nb * seqlen, -1)
    ps_all = p_sample.reshape(depth, db * n_new, -1)

    h_p = x_prompt
    h_s = x_sample.reshape(1, db * n_new, d)
    outs = {k: [] for k in ('k_p', 'v_p', 'lf_p', 're_p', 'im_p', 'k_s', 'v_s', 'lf_s', 're_s', 'im_s')}
    for l in range(depth):
        q, k, v, kb, vb, u_c, lf, lft = _in_proj(h_p, pw, l, n_heads=n_heads, tm=ROW_TILE, chunk=chunk_p)
        negc = _negcumsum(lft.reshape(nb * n_heads, seqlen)).reshape(nb, n_heads, seqlen)
        attn = _attn_prompt(q, kb, vb, negc, n_heads=n_heads, tile=ATTN_TILE)
        y_c, x_last = _ssm(u_c, jnp.zeros((nb, n_groups * 2 * SSM_STATE), F32), tables_p, scan_p, l,
                           cb=SSM_CHUNKS_PER_STEP)
        h_p = _out_ffn(h_p.reshape(nb * seqlen, d), attn.reshape(nb * seqlen, aw),
                       y_c.reshape(nb * seqlen // chunk_p, -1), pp_all, pw, l,
                       tm=ROW_TILE, chunk=chunk_p).reshape(nb, seqlen, d)
        re, im = _unpack_state(x_last, n_groups)
        outs['k_p'].append(k.reshape(nb, seqlen, n_heads, HEAD_DIM))
        outs['v_p'].append(v.reshape(nb, seqlen, n_heads, HEAD_DIM))
        outs['lf_p'].append(jnp.swapaxes(lft, 1, 2))
        outs['re_p'].append(re)
        outs['im_p'].append(im)

        q, k, v, _, _, u, lf, _ = _in_proj(h_s, pw, l, n_heads=n_heads, tm=ROW_TILE, chunk=0)
        rs = lambda a: a.reshape(db, n_new, a.shape[-1])
        pad_page = lambda a: jnp.pad(rs(a), ((0, 0), (0, page - n_new), (0, 0)))
        lf_pad = jnp.pad(rs(lf), ((0, 0), (0, 0), (0, LANES - n_heads)))
        attn = _attn_sample(rs(q), pad_page(k), pad_page(v), lf_pad, ck, cv, clft, page_table, l,
                            n_heads=n_heads, pp=DEC_PAGES_PER_STEP)
        y_c, x_last = _ssm(_to_chunk_row(rs(u)), _pack_state(state_ssm_re[l], state_ssm_im[l]),
                           tables_s, scan_s, l, cb=1)
        h_s = _out_ffn(h_s.reshape(db * n_new, d), attn.reshape(db * n_new, aw),
                       _from_chunk_row(y_c, n_new).reshape(db * n_new, -1), ps_all, pw, l,
                       tm=ROW_TILE, chunk=0).reshape(1, db * n_new, d)
        re, im = _unpack_state(x_last, n_groups)
        outs['k_s'].append(k.reshape(db, n_new, n_heads, HEAD_DIM))
        outs['v_s'].append(v.reshape(db, n_new, n_heads, HEAD_DIM))
        outs['lf_s'].append(lf.reshape(db, n_new, n_heads))
        outs['re_s'].append(re)
        outs['im_s'].append(im)

    st = lambda name: jnp.stack(outs[name])
    return (h_p, h_s.reshape(db, n_new, d), st('k_p'), st('v_p'), st('lf_p'), st('re_p'), st('im_p'),
            st('k_s'), st('v_s'), st('lf_s'), st('re_s'), st('im_s'))
```

```python
import functools
import math

import numpy as np
import jax
import jax.numpy as jnp
from jax import lax
from jax.experimental import pallas as pl
from jax.experimental.pallas import tpu as pltpu

F32 = jnp.float32
BF16 = jnp.bfloat16

RMS_EPS = 1e-6
NEG_INF = -1e30
LOG2E = math.log2(math.e)
HEAD_DIM = 64
SSM_GROUP = 16
SSM_STATE = 64
LANES = 128
MXU_DIM = 256
VMEM_LIMIT_BYTES = 56 * 1024 * 1024

GROUPS_PER_BLOCK = LANES // SSM_GROUP
HALF_STATE = GROUPS_PER_BLOCK * SSM_STATE
BLOCK_STATE = 2 * HALF_STATE

PROMPT_CHUNK = 16
SSM_CHUNKS_PER_STEP = 32
ROW_TILE = 512
ATTN_TILE = 512
DIAG_SUB = 512
DEC_PAGES_PER_STEP = 16
DEC_CHAINS = 1
FF_CHUNK = 256


def _params(semantics):
    return pltpu.CompilerParams(dimension_semantics=semantics, vmem_limit_bytes=VMEM_LIMIT_BYTES)


def _rms(x, g):
    return x * lax.rsqrt(jnp.mean(x * x, axis=-1, keepdims=True) + RMS_EPS) * g


def _sigmoid(x):
    return 1.0 / (1.0 + jnp.exp(-x))


def _log_sigmoid(x):
    return jnp.minimum(x, 0.0) - jnp.log1p(jnp.exp(-jnp.abs(x)))


def _gelu_tanh(x):
    c = math.sqrt(2.0 / math.pi)
    return 0.5 * x * (1.0 + jnp.tanh(c * (x + 0.044715 * (x * x * x))))


def _split3(x):
    h1 = x.astype(BF16)
    r1 = x - h1.astype(F32)
    h2 = r1.astype(BF16)
    h3 = (r1 - h2.astype(F32)).astype(BF16)
    return h1, h2, h3


def _dot(a, b):
    return jnp.dot(a, b, preferred_element_type=F32)


def _dot_nt(a, b):
    return lax.dot_general(a, b, (((1,), (1,)), ((), ())), preferred_element_type=F32)


def _dot3(x, w):
    h1, h2, h3 = _split3(x)
    return _dot(h1, w) + _dot(h2, w) + _dot(h3, w)


def _dot_nt_split(a, b):
    a1, a2, _ = _split3(a)
    b1, b2, _ = _split3(b)
    return _dot_nt(a1, b1) + _dot_nt(a1, b2) + _dot_nt(a2, b1)


def _const_spec(shape):
    nd = len(shape)
    return pl.BlockSpec(shape, lambda *_: (0,) * nd)


def _layer_spec(arr, layer, **kw):
    zeros = (0,) * (arr.ndim - 1)
    return pl.BlockSpec((None,) + arr.shape[1:], lambda *_: (layer,) + zeros, **kw)


def _rows_strided(start, size, stride):
    return pl.ds(start, size) if stride == 1 else pl.ds(start, size, stride=stride)


def _in_proj_kernel(x_ref, g_ref, w_ref, bf_ref, gqk_ref, bd_ref,
                    q_ref, k_ref, v_ref, kb_ref, vb_ref, u_ref, lf_ref, lft_ref, *scratch,
                    n_heads, aw, sw, chunk):
    n = _rms(x_ref[0], g_ref[...]).astype(BF16)
    proj = _dot(n, w_ref[...])
    bd = bd_ref[...]
    parts = []
    for c in range(2 * aw // MXU_DIM):
        blk = proj[:, c * MXU_DIM:(c + 1) * MXU_DIM]
        ssq = _dot((blk * blk).astype(BF16), bd)
        parts.append(blk * lax.rsqrt(ssq * (1.0 / HEAD_DIM) + RMS_EPS))
    qkn = jnp.concatenate(parts, axis=1) * gqk_ref[...]
    q_ref[0] = (qkn[:, :aw] * (LOG2E * HEAD_DIM ** -0.5)).astype(BF16)
    k = qkn[:, aw:]
    k_ref[0] = k
    kb_ref[0] = k.astype(BF16)
    v = proj[:, 2 * aw:3 * aw]
    v_ref[0] = v
    vb_ref[0] = v.astype(BF16)
    u = proj[:, 3 * aw:3 * aw + sw]
    if chunk:
        u_scr, = scratch
        rows_c = u.shape[0] // chunk
        for j in range(sw // LANES):
            u_scr[j] = u[:, j * LANES:(j + 1) * LANES]
            for t in range(chunk):
                piece = u_scr[j, pl.ds(t, rows_c, stride=chunk), :]
                u_ref[0, :, (j * chunk + t) * LANES:(j * chunk + t + 1) * LANES] = piece.astype(BF16)
    else:
        u_ref[0] = u.astype(BF16)
    lf = _log_sigmoid(proj[:, 3 * aw + sw:] + bf_ref[...])
    lf_ref[0] = lf[:, :n_heads]
    lft_ref[0] = lf.T[:n_heads, :]


def _in_proj(x, pw, layer, *, n_heads, tm, chunk):
    nb, seqlen, d = x.shape
    aw = n_heads * HEAD_DIM
    sw = pw['w_in'].shape[2] - 3 * aw - LANES
    tm = min(tm, seqlen)
    grid = (nb, seqlen // tm)
    row = lambda w: pl.BlockSpec((1, tm, w), lambda b, i: (b, i, 0))
    if chunk:
        u_shape = jax.ShapeDtypeStruct((nb, seqlen // chunk, chunk * sw), BF16)
        u_spec = pl.BlockSpec((1, tm // chunk, chunk * sw), lambda b, i: (b, i, 0))
        scratch = [pltpu.VMEM((sw // LANES, tm, LANES), F32)]
    else:
        u_shape = jax.ShapeDtypeStruct((nb, seqlen, sw), BF16)
        u_spec = row(sw)
        scratch = []
    out_shape = (
        jax.ShapeDtypeStruct((nb, seqlen, aw), BF16),
        jax.ShapeDtypeStruct((nb, seqlen, aw), F32),
        jax.ShapeDtypeStruct((nb, seqlen, aw), F32),
        jax.ShapeDtypeStruct((nb, seqlen, aw), BF16),
        jax.ShapeDtypeStruct((nb, seqlen, aw), BF16),
        u_shape,
        jax.ShapeDtypeStruct((nb, seqlen, n_heads), F32),
        jax.ShapeDtypeStruct((nb, n_heads, seqlen), F32),
    )
    out_specs = (row(aw), row(aw), row(aw), row(aw), row(aw), u_spec,
                 pl.BlockSpec((1, tm, n_heads), lambda b, i: (b, i, 0)),
                 pl.BlockSpec((1, n_heads, tm), lambda b, i: (b, 0, i)))
    weights = [pw['g_mix'], pw['w_in'], pw['b_f'], pw['g_qk']]
    in_specs = [row(d)] + [_layer_spec(a, layer) for a in weights] + [_const_spec(pw['bd'].shape)]
    return pl.pallas_call(
        functools.partial(_in_proj_kernel, n_heads=n_heads, aw=aw, sw=sw, chunk=chunk),
        grid=grid, in_specs=in_specs, out_specs=out_specs, out_shape=out_shape, scratch_shapes=scratch,
        compiler_params=_params(("parallel", "parallel")), name="in_proj",
    )(x, *weights, pw['bd'])


def _negcumsum_kernel(x_ref, tri_ref, ones_ref, o_ref):
    rows, seqlen = x_ref.shape
    tri = tri_ref[...]
    ones = ones_ref[...]
    carry = jnp.zeros((rows, LANES), F32)
    for j in range(seqlen // LANES):
        sl = slice(j * LANES, (j + 1) * LANES)
        xc = x_ref[:, sl]
        o_ref[:, sl] = -LOG2E * (_dot3(xc, tri) + carry)
        carry = carry + _dot3(xc, ones)


def _negcumsum(x):
    tri = jnp.asarray(np.triu(np.ones((LANES, LANES), np.float32)), BF16)
    ones = jnp.ones((LANES, LANES), BF16)
    return pl.pallas_call(
        _negcumsum_kernel, grid=(1,),
        in_specs=[_const_spec(x.shape), _const_spec(tri.shape), _const_spec(ones.shape)],
        out_specs=_const_spec(x.shape), out_shape=jax.ShapeDtypeStruct(x.shape, F32),
        compiler_params=_params(("arbitrary",)), name="negcumsum",
    )(x, tri, ones)


def _attn_kernel(qi_ref, ki_ref, q_ref, k_ref, v_ref, nc_ref, o_ref, *scratch, n_heads, tq, tk):
    step = pl.program_id(1)
    qi = qi_ref[step]
    ki = ki_ref[step]
    lane = lax.broadcasted_iota(jnp.int32, (1, LANES), 1)
    zero = jnp.zeros((), BF16)
    m_scr, acc_scr = scratch[:n_heads], scratch[n_heads:]

    def own_lanes(h):
        return (lane < HEAD_DIM) if h % 2 == 0 else (lane >= HEAD_DIM)

    def denom_lane(h):
        return HEAD_DIM if h % 2 == 0 else 0

    @pl.when(ki == 0)
    def _():
        for h in range(n_heads):
            m_scr[h][...] = jnp.full(m_scr[h].shape, NEG_INF, F32)
            acc_scr[h][...] = jnp.zeros(acc_scr[h].shape, F32)

    def head_block(h, r0, nr, nk, masked):
        psl = slice((h // 2) * LANES, (h // 2 + 1) * LANES)
        rows = slice(r0, r0 + nr)
        own = own_lanes(h)
        qh = jnp.where(own, q_ref[0, rows, psl], zero)
        s = _dot_nt(qh, k_ref[0, :nk, psl]) + nc_ref[0, h:h + 1, :nk]
        if masked:
            rowi = lax.broadcasted_iota(jnp.int32, (nr, nk), 0) + r0
            coli = lax.broadcasted_iota(jnp.int32, (nr, nk), 1)
            s = jnp.where(coli <= rowi, s, NEG_INF)
        m_prev = m_scr[h][rows]
        m_new = jnp.maximum(m_prev, jnp.max(s, axis=1, keepdims=True))
        p = jnp.exp2(s - jnp.tile(m_new, (1, nk // LANES)))
        m_scr[h][rows] = m_new
        vh = jnp.where(own, v_ref[0, :nk, psl], jnp.where(lane == denom_lane(h), 1.0, 0.0).astype(BF16))
        acc_scr[h][rows] = acc_scr[h][rows] * jnp.exp2(m_prev - m_new) + _dot(p.astype(BF16), vh)

    @pl.when(ki < qi)
    def _():
        for h in range(n_heads):
            head_block(h, 0, tq, tk, False)

    @pl.when(ki == qi)
    def _():
        sub = min(tq, DIAG_SUB)
        for h in range(n_heads):
            for r0 in range(0, tq, sub):
                head_block(h, r0, sub, r0 + sub, True)
        for pair in range(n_heads // 2):
            halves = []
            for h in (2 * pair, 2 * pair + 1):
                acc = acc_scr[h][...]
                halves.append(acc * (1.0 / acc[:, denom_lane(h):denom_lane(h) + 1]))
            out = jnp.where(own_lanes(0), halves[0], halves[1])
            o_ref[0, :, pair * LANES:(pair + 1) * LANES] = out.astype(o_ref.dtype)


def _attn_prompt(q, kb, vb, negc, *, n_heads, tile):
    nb, seqlen, aw = q.shape
    t = min(tile, seqlen)
    nq = seqlen // t
    pairs = [(i, j) for i in range(nq) for j in range(i + 1)]
    qi_tab = jnp.asarray([p[0] for p in pairs], jnp.int32)
    ki_tab = jnp.asarray([p[1] for p in pairs], jnp.int32)
    grid_spec = pltpu.PrefetchScalarGridSpec(
        num_scalar_prefetch=2, grid=(nb, len(pairs)),
        in_specs=[
            pl.BlockSpec((1, t, aw), lambda b, s, qi, ki: (b, qi[s], 0)),
            pl.BlockSpec((1, t, aw), lambda b, s, qi, ki: (b, ki[s], 0)),
            pl.BlockSpec((1, t, aw), lambda b, s, qi, ki: (b, ki[s], 0)),
            pl.BlockSpec((1, n_heads, t), lambda b, s, qi, ki: (b, 0, ki[s])),
        ],
        out_specs=pl.BlockSpec((1, t, aw), lambda b, s, qi, ki: (b, qi[s], 0)),
        scratch_shapes=[pltpu.VMEM((t, LANES), F32)] * (2 * n_heads),
    )
    return pl.pallas_call(
        functools.partial(_attn_kernel, n_heads=n_heads, tq=t, tk=t),
        grid_spec=grid_spec, out_shape=jax.ShapeDtypeStruct((nb, seqlen, aw), BF16),
        compiler_params=_params(("parallel", "arbitrary")), name="attn",
    )(qi_tab, ki_tab, q, kb, vb, negc)


def _dec_attn_kernel(pt_ref, q_ref, kn_ref, vn_ref, lfn_ref, tri_ref, ones_ref, ck_hbm, cv_hbm, clf_hbm,
                     o_ref, kbuf, vbuf, lfbuf, sem, qbd_scr, carry_scr, *state,
                     pp, n_heads, n_new, layer, n_pages):
    n_chains = len(state) // 3
    m_scr, l_scr, acc_scr = state[:n_chains], state[n_chains:2 * n_chains], state[2 * n_chains:]
    b = pl.program_id(0)
    j = pl.program_id(1)
    nj = pl.num_programs(1)
    step = b * nj + j
    slot = lax.rem(step, 2)
    rows = n_heads * n_new
    aw = n_heads * HEAD_DIM
    page = kbuf.shape[-1]

    def page_copies(seq, grp, half, lookup):
        out = []
        for i in range(pp):
            pg = pt_ref[seq, n_pages - 1 - (grp * pp + i)] if lookup else 0
            out.append(pltpu.make_async_copy(ck_hbm.at[layer, pg], kbuf.at[half, i], sem.at[0, half]))
            out.append(pltpu.make_async_copy(cv_hbm.at[layer, pg], vbuf.at[half, i], sem.at[1, half]))
            out.append(pltpu.make_async_copy(clf_hbm.at[layer, pg], lfbuf.at[half, i], sem.at[2, half]))
        return out

    @pl.when(step == 0)
    def _():
        for cp in page_copies(0, 0, 0, True):
            cp.start()

    @pl.when(step + 1 < pl.num_programs(0) * nj)
    def _():
        nxt = step + 1
        for cp in page_copies(nxt // nj, lax.rem(nxt, nj), 1 - slot, True):
            cp.start()

    for cp in page_copies(b, j, slot, False):
        cp.wait()

    def expand(r):
        return jnp.broadcast_to(r[:, None, :], (n_heads, n_new, r.shape[-1])).reshape(rows, r.shape[-1])

    def page_t(buf, i):
        return buf[slot, i].reshape(aw, page).astype(BF16)

    @pl.when(j == 0)
    def _():
        qt = jnp.concatenate([q_ref[0].astype(F32)] * n_heads, axis=0)
        rowi = lax.broadcasted_iota(jnp.int32, (rows, aw), 0)
        lani = lax.broadcasted_iota(jnp.int32, (rows, aw), 1)
        qbd = jnp.where((lani // HEAD_DIM) == (rowi // n_new), qt, 0.0)
        qbd_scr[...] = qbd.astype(BF16)
        lfn = lfn_ref[0]
        run = jnp.zeros((1, LANES), F32)
        crow = []
        for t in range(n_new):
            run = run + lfn[t:t + 1, :]
            crow.append(run)
        cnew = jnp.concatenate(crow + [jnp.zeros((LANES - n_new, LANES), F32)], axis=0)
        bias = expand(-LOG2E * cnew.T[:n_heads, :])
        s = _dot_nt(qbd_scr[...], kn_ref[0].astype(BF16)) + bias
        rown = lax.broadcasted_iota(jnp.int32, (rows, LANES), 0)
        coln = lax.broadcasted_iota(jnp.int32, (rows, LANES), 1)
        s = jnp.where(coln <= (rown % n_new), s, NEG_INF)
        m = jnp.max(s, axis=1, keepdims=True)
        p = jnp.exp2(s - m)
        for c in range(1, n_chains):
            m_scr[c][...] = jnp.full(m_scr[c].shape, NEG_INF, F32)
            l_scr[c][...] = jnp.zeros(l_scr[c].shape, F32)
            acc_scr[c][...] = jnp.zeros(acc_scr[c].shape, F32)
        m_scr[0][...] = jnp.broadcast_to(m, m_scr[0].shape)
        l_scr[0][...] = jnp.broadcast_to(jnp.sum(p, axis=1, keepdims=True), l_scr[0].shape)
        acc_scr[0][...] = _dot(p.astype(BF16), vn_ref[0].astype(BF16))
        carry_scr[...] = jnp.zeros(carry_scr.shape, F32)

    x = lfbuf[slot].reshape(pp * n_heads, page)
    within = _dot3(x, tri_ref[...])
    tot = _dot3(x, ones_ref[...])
    carry = carry_scr[...]
    qbd = qbd_scr[...]
    biases = []
    for i in range(pp):
        sl = slice(i * n_heads, (i + 1) * n_heads)
        biases.append(expand(LOG2E * (within[sl] + carry)))
        carry = carry + tot[sl]
    carry_scr[...] = carry
    ppc = pp // n_chains
    for c in range(n_chains):
        pages = range(c * ppc, (c + 1) * ppc)
        s = jnp.concatenate([_dot(qbd, page_t(kbuf, i)) + biases[i] for i in pages], axis=1)
        m_prev = m_scr[c][...]
        m_new = jnp.maximum(m_prev, jnp.max(s, axis=1, keepdims=True))
        alpha = jnp.exp2(m_prev - m_new)
        p = jnp.exp2(s - jnp.tile(m_new, (1, ppc * page // LANES)))
        l_scr[c][...] = alpha * l_scr[c][...] + jnp.sum(p, axis=1, keepdims=True)
        m_scr[c][...] = m_new
        pv = None
        for n, i in enumerate(pages):
            d = _dot_nt(p[:, n * page:(n + 1) * page].astype(BF16), page_t(vbuf, i))
            pv = d if pv is None else pv + d
        acc_scr[c][...] = acc_scr[c][...] * jnp.tile(alpha, (1, aw // LANES)) + pv

    @pl.when(j == pl.num_programs(1) - 1)
    def _():
        m = m_scr[0][...]
        for c in range(1, n_chains):
            m = jnp.maximum(m, m_scr[c][...])
        l = jnp.zeros(m.shape, F32)
        acc = jnp.zeros(acc_scr[0].shape, F32)
        for c in range(n_chains):
            wgt = jnp.exp2(m_scr[c][...] - m)
            l = l + wgt * l_scr[c][...]
            acc = acc + jnp.tile(wgt, (1, aw // LANES)) * acc_scr[c][...]
        o = acc * jnp.tile(1.0 / l, (1, aw // LANES))
        lano = lax.broadcasted_iota(jnp.int32, (n_new, aw), 1)
        out = jnp.zeros((n_new, aw), F32)
        for h in range(n_heads):
            out = jnp.where((lano // HEAD_DIM) == h, o[h * n_new:(h + 1) * n_new, :], out)
        o_ref[0] = out.astype(o_ref.dtype)


def _attn_sample(q, k_new_pad, v_new_pad, lf_new, cache_k, cache_v, cache_lft, page_table, layer,
                 *, n_heads, pp):
    nb, n_new, aw = q.shape
    n_pages = page_table.shape[1]
    page = cache_k.shape[-1]
    pp = min(pp, n_pages)
    nj = n_pages // pp
    chains = DEC_CHAINS if pp % DEC_CHAINS == 0 else 1
    rows = n_heads * n_new
    tri = jnp.asarray(np.tril(np.ones((page, page), np.float32), -1), BF16)
    ones = jnp.ones((page, page), BF16)

    per_b = lambda shape: pl.BlockSpec((1,) + shape, lambda b, j, pt: (b, 0, 0))
    const = lambda shape: pl.BlockSpec(shape, lambda b, j, pt: (0, 0))
    hbm = pl.BlockSpec(memory_space=pl.ANY)
    in_specs = [per_b((n_new, aw)), per_b((page, aw)), per_b((page, aw)), per_b((n_new, LANES)),
                const(tri.shape), const(ones.shape), hbm, hbm, hbm]
    grid_spec = pltpu.PrefetchScalarGridSpec(
        num_scalar_prefetch=1, grid=(nb, nj), in_specs=in_specs,
        out_specs=pl.BlockSpec((1, n_new, aw), lambda b, j, pt: (b, 0, 0)),
        scratch_shapes=[pltpu.VMEM((2, pp, n_heads, HEAD_DIM, page), F32),
                        pltpu.VMEM((2, pp, n_heads, HEAD_DIM, page), F32),
                        pltpu.VMEM((2, pp, n_heads, page), F32),
                        pltpu.SemaphoreType.DMA((3, 2)),
                        pltpu.VMEM((rows, aw), BF16), pltpu.VMEM((n_heads, page), F32)]
                       + [pltpu.VMEM((rows, LANES), F32)] * (2 * chains)
                       + [pltpu.VMEM((rows, aw), F32)] * chains,
    )
    return pl.pallas_call(
        functools.partial(_dec_attn_kernel, pp=pp, n_heads=n_heads, n_new=n_new, layer=layer, n_pages=n_pages),
        grid_spec=grid_spec, out_shape=jax.ShapeDtypeStruct((nb, n_new, aw), BF16),
        compiler_params=_params(("arbitrary", "arbitrary")), name="dec_attn",
    )(page_table, q, k_new_pad, v_new_pad, lf_new, tri, ones, cache_k, cache_v, cache_lft)


def _ssm_block_params(lam_re, lam_im, log_dt, b_re, b_im, c_re, c_im, d_skip, t_max):
    depth, g, p = lam_re.shape
    gpb = GROUPS_PER_BLOCK
    nj = g // gpb
    lam = lax.complex(lam_re.astype(F32), lam_im.astype(F32))
    lam_dt = lam * jnp.exp(log_dt.astype(F32))[..., None]
    lam_bar = jnp.exp(lam_dt)
    b_bar = ((lam_bar - 1.0) / lam)[..., None] * lax.complex(b_re.astype(F32), b_im.astype(F32))
    c = lax.complex(c_re.astype(F32), c_im.astype(F32))
    steps = jnp.arange(t_max + 1, dtype=F32).astype(jnp.complex64)
    pw = jnp.exp(lam_dt[:, :, None, :] * steps[None, None, :, None])
    pw = pw.reshape(depth, nj, gpb, t_max + 1, p).transpose(0, 1, 3, 2, 4).reshape(depth, nj, t_max + 1, gpb * p)
    eye = jnp.asarray(np.eye(gpb, dtype=bool))[None, None, :, None, :, None]

    def block_diag(a):
        a = jnp.where(eye, a[:, :, :, :, None, :], 0.0)
        return a.reshape(depth, nj, gpb * SSM_GROUP, gpb * p)

    bmat = block_diag(b_bar.transpose(0, 1, 3, 2).reshape(depth, nj, gpb, SSM_GROUP, p))
    cmat = block_diag(c.reshape(depth, nj, gpb, SSM_GROUP, p))
    ri = lambda z: (jnp.real(z), jnp.imag(z))
    dsk = d_skip.astype(F32).reshape(depth, 1, g * SSM_GROUP)
    return ri(bmat), ri(cmat), ri(pw), dsk


def _ssm_tables_kernel(bre_ref, bim_ref, cre_ref, cim_ref, pwr_ref, pwi_ref, mt_ref, bt_ref, ct_ref, *, t):
    bre, bim = bre_ref[...], bim_ref[...]
    cre, cim = cre_ref[...], cim_ref[...]
    half = lambda i: slice(i * LANES, (i + 1) * LANES)
    lo, hi = slice(0, HALF_STATE), slice(HALF_STATE, BLOCK_STATE)
    bd = []
    for e in range(t):
        pr, pi = pwr_ref[e:e + 1, :], pwi_ref[e:e + 1, :]
        xr = bre * pr - bim * pi
        xi = bre * pi + bim * pr
        s = t - 1 - e
        bt_ref[s // 2, half(s % 2), lo] = xr.astype(BF16)
        bt_ref[s // 2, half(s % 2), hi] = xi.astype(BF16)
        bd.append(_dot_nt_split(xr, cre) - _dot_nt_split(xi, cim))
        pr, pi = pwr_ref[e + 1:e + 2, :], pwi_ref[e + 1:e + 2, :]
        ct_ref[e // 2, half(e % 2), lo] = (cre * pr - cim * pi).astype(BF16)
        ct_ref[e // 2, half(e % 2), hi] = (-(cre * pi + cim * pr)).astype(BF16)
    zero = jnp.zeros((LANES, LANES), BF16)
    for d in range(t // 2):
        mt_ref[d, half(0), half(0)] = bd[2 * d].astype(BF16)
        mt_ref[d, half(0), half(1)] = bd[2 * d + 1].astype(BF16)
        mt_ref[d, half(1), half(0)] = bd[2 * d - 1].astype(BF16) if d else zero
        mt_ref[d, half(1), half(1)] = bd[2 * d].astype(BF16)


def _ssm_tables(block_params, t):
    (bre, bim), (cre, cim), (pwr, pwi), _ = block_params
    depth, nj = bre.shape[:2]
    nt = t // 2
    mat = pl.BlockSpec((None, None) + bre.shape[2:], lambda l, j: (l, j, 0, 0))
    pws = pl.BlockSpec((None, None) + pwr.shape[2:], lambda l, j: (l, j, 0, 0))
    tile = lambda a, b: pl.BlockSpec((None, None, nt, a, b), lambda l, j: (l, j, 0, 0, 0))
    shape = lambda a, b: jax.ShapeDtypeStruct((depth, nj, nt, a, b), BF16)
    return pl.pallas_call(
        functools.partial(_ssm_tables_kernel, t=t), grid=(depth, nj),
        in_specs=[mat, mat, mat, mat, pws, pws],
        out_specs=(tile(MXU_DIM, MXU_DIM), tile(MXU_DIM, BLOCK_STATE), tile(MXU_DIM, BLOCK_STATE)),
        out_shape=(shape(MXU_DIM, MXU_DIM), shape(MXU_DIM, BLOCK_STATE), shape(MXU_DIM, BLOCK_STATE)),
        compiler_params=_params(("parallel", "parallel")), name="ssm_tables",
    )(bre, bim, cre, cim, pwr, pwi)


def _ssm_scan_params(block_params, t):
    _, _, (pwr, pwi), dsk = block_params
    depth, nj = pwr.shape[:2]
    ar, ai = pwr[:, :, t], pwi[:, :, t]
    a1 = jnp.concatenate([ar, ar], axis=-1).reshape(depth, 1, nj * BLOCK_STATE)
    a2 = jnp.concatenate([-ai, ai], axis=-1).reshape(depth, 1, nj * BLOCK_STATE)
    dsk_c = jnp.broadcast_to(dsk.reshape(depth, nj, 1, LANES), (depth, nj, t, LANES)).reshape(depth, 1, nj * t * LANES)
    return a1, a2, dsk_c


def _ssm_kernel(u_ref, x0_ref, a1_ref, a2_ref, dsk_ref, bt_ref, mt_ref, ct_ref, y_ref, xl_ref,
                s_scr, xin_scr, x_scr, *, nb, cb, nt):
    rows = nb * cb
    n_lt = BLOCK_STATE // LANES
    u = u_ref[...].reshape(rows, nt * MXU_DIM)
    tile = lambda a: slice(a * MXU_DIM, (a + 1) * MXU_DIM)
    lt = lambda i: slice(i * LANES, (i + 1) * LANES)

    @pl.when(pl.program_id(1) == 0)
    def _():
        x_scr[...] = x0_ref[...]

    s = _dot(u[:, tile(0)], bt_ref[0])
    for b in range(1, nt):
        s = s + _dot(u[:, tile(b)], bt_ref[b])
    for i in range(n_lt):
        s_scr[i] = s[:, lt(i)]
    a1 = jnp.broadcast_to(a1_ref[...], (nb, BLOCK_STATE))
    a2 = jnp.broadcast_to(a2_ref[...], (nb, BLOCK_STATE))

    def step(c, x):
        sel = _rows_strided(c, nb, cb)
        for i in range(n_lt):
            xin_scr[i, sel, :] = x[:, lt(i)]
        swapped = jnp.concatenate([x[:, HALF_STATE:], x[:, :HALF_STATE]], axis=1)
        s_c = jnp.concatenate([s_scr[i, sel, :] for i in range(n_lt)], axis=1)
        return a1 * x + a2 * swapped + s_c

    x = lax.fori_loop(0, cb, step, x_scr[...])
    x_scr[...] = x
    xl_ref[...] = x
    xin = jnp.concatenate([xin_scr[i] for i in range(n_lt)], axis=1).astype(BF16)
    for a in range(nt):
        acc = _dot_nt(xin, ct_ref[a])
        for b in range(a + 1):
            acc = acc + _dot(u[:, tile(b)], mt_ref[a - b])
        acc = acc + dsk_ref[:, tile(a)] * u[:, tile(a)].astype(F32)
        if len(y_ref.shape) == 3:
            y_ref[:, :, tile(a)] = acc.reshape(nb, cb, MXU_DIM)
        else:
            y_ref[:, tile(a)] = acc


def _ssm(u_c, x0, tables, scan_params, layer, *, cb):
    mt, bt, ct = tables
    a1, a2, dsk = scan_params
    nj, nt = mt.shape[1], mt.shape[2]
    w = nt * MXU_DIM
    nb = u_c.shape[0]
    if u_c.ndim == 3:
        n_chunks = u_c.shape[1]
        cb = min(cb, n_chunks)
        io_spec = pl.BlockSpec((nb, cb, w), lambda j, r: (0, r, j))
    else:
        n_chunks = cb = 1
        io_spec = pl.BlockSpec((nb, w), lambda j, r: (0, j))
    state = pl.BlockSpec((nb, BLOCK_STATE), lambda j, r: (0, j))
    vec = lambda width: pl.BlockSpec((None, 1, width), lambda j, r: (layer, 0, j))
    tab = lambda a: pl.BlockSpec((None, None) + a.shape[2:], lambda j, r: (layer, j, 0, 0, 0))
    rows = nb * cb
    n_lt = BLOCK_STATE // LANES
    return pl.pallas_call(
        functools.partial(_ssm_kernel, nb=nb, cb=cb, nt=nt),
        grid=(nj, n_chunks // cb),
        in_specs=[io_spec, state, vec(BLOCK_STATE), vec(BLOCK_STATE), vec(w), tab(bt), tab(mt), tab(ct)],
        out_specs=(io_spec, state),
        out_shape=(jax.ShapeDtypeStruct(u_c.shape, F32),
                   jax.ShapeDtypeStruct((nb, nj * BLOCK_STATE), F32)),
        scratch_shapes=[pltpu.VMEM((n_lt, rows, LANES), F32), pltpu.VMEM((n_lt, rows, LANES), F32),
                        pltpu.VMEM((nb, BLOCK_STATE), F32)],
        compiler_params=_params(("parallel", "arbitrary")), name="ssm",
    )(u_c, x0, a1, a2, dsk, bt, mt, ct)


def _to_chunk_row(u):
    nb, t, w = u.shape
    return u.reshape(nb, t, w // LANES, LANES).transpose(0, 2, 1, 3).reshape(nb, t * w)


def _from_chunk_row(y, t):
    nb, tw = y.shape
    w = tw // t
    return y.reshape(nb, w // LANES, t, LANES).transpose(0, 2, 1, 3).reshape(nb, t, w)


def _pack_state(re, im):
    nb, g, p = re.shape
    nj = g // GROUPS_PER_BLOCK
    x = jnp.stack([re.astype(F32).reshape(nb, nj, HALF_STATE), im.astype(F32).reshape(nb, nj, HALF_STATE)], axis=2)
    return x.reshape(nb, nj * BLOCK_STATE)


def _unpack_state(x, n_groups):
    nb = x.shape[0]
    x = x.reshape(nb, n_groups // GROUPS_PER_BLOCK, 2, GROUPS_PER_BLOCK, SSM_STATE)
    return x[:, :, 0].reshape(nb, n_groups, SSM_STATE), x[:, :, 1].reshape(nb, n_groups, SSM_STATE)


def _out_ffn_kernel(h_ref, a_ref, y_ref, p_ref, wglu_ref, bglu_ref, ga_ref, gs_ref,
                    wo_ref, gffn_ref, w1_ref, w3_ref, w2_ref, gple_ref, wpg_ref, bpg_ref, wpe_ref,
                    o_ref, *scratch, ff_chunk, chunk):
    if chunk:
        y_scr, = scratch
        rows_c = y_ref.shape[0]
        for j in range(y_scr.shape[0]):
            for t in range(chunk):
                y_scr[j, pl.ds(t, rows_c, stride=chunk), :] = (
                    y_ref[:, (j * chunk + t) * LANES:(j * chunk + t + 1) * LANES])
        y = jnp.concatenate([y_scr[j] for j in range(y_scr.shape[0])], axis=1)
    else:
        y = y_ref[...]
    y = _gelu_tanh(y)
    ssm = y * _sigmoid(_dot(y.astype(BF16), wglu_ref[...]) + bglu_ref[...])
    na = _rms(a_ref[...].astype(F32), ga_ref[...]).astype(BF16)
    ns = _rms(ssm, gs_ref[...]).astype(BF16)
    aw = na.shape[1]
    h = h_ref[...] + _dot(na, wo_ref[:aw, :]) + _dot(ns, wo_ref[aw:, :])
    n2 = _rms(h, gffn_ref[...]).astype(BF16)
    d_ff = w1_ref.shape[1]
    ff = jnp.zeros(h.shape, F32)
    for c in range(d_ff // ff_chunk):
        sl = slice(c * ff_chunk, (c + 1) * ff_chunk)
        a = _dot(n2, w1_ref[:, sl])
        b = _dot(n2, w3_ref[:, sl])
        ff = ff + _dot((a * _sigmoid(a) * b).astype(BF16), w2_ref[sl, :])
    h = h + ff
    n3 = _rms(h, gple_ref[...]).astype(BF16)
    gate = _sigmoid(_dot(n3, wpg_ref[...]) + bpg_ref[...])
    o_ref[...] = h + gate * _dot(p_ref[...].astype(BF16), wpe_ref[...])


def _out_ffn(h, attn, yssm, p_all, pw, layer, *, tm, chunk):
    m, d = h.shape
    tm = min(tm, m)
    sw = pw['w_glu'].shape[1]
    row = lambda w: pl.BlockSpec((tm, w), lambda i: (i, 0))
    weights = [pw[k] for k in ('w_glu', 'b_glu', 'g_attn_out', 'g_ssm_out', 'w_out', 'g_ffn', 'w_ff1', 'w_ff3',
                               'w_ff2', 'g_ple', 'w_pg', 'b_pg', 'w_pe')]
    d_ff = pw['w_ff1'].shape[2]
    ff_chunk = FF_CHUNK if d_ff % FF_CHUNK == 0 else d_ff
    if chunk:
        y_spec = pl.BlockSpec((tm // chunk, chunk * sw), lambda i: (i, 0))
        scratch = [pltpu.VMEM((sw // LANES, tm, LANES), F32)]
    else:
        y_spec = row(sw)
        scratch = []
    p_spec = pl.BlockSpec((None, tm, p_all.shape[2]), lambda i: (layer, i, 0))
    return pl.pallas_call(
        functools.partial(_out_ffn_kernel, ff_chunk=ff_chunk, chunk=chunk),
        grid=(m // tm,),
        in_specs=[row(d), row(attn.shape[1]), y_spec, p_spec]
                 + [_layer_spec(a, layer, pipeline_mode=pl.Buffered(1)) for a in weights],
        out_specs=row(d), out_shape=jax.ShapeDtypeStruct((m, d), F32), scratch_shapes=scratch,
        compiler_params=_params(("parallel",)), name="out_ffn",
    )(h, attn, yssm, p_all, *weights)


def _prep_weights(n_heads, aw, w):
    depth = w['w_in'].shape[0]
    row = lambda a: a.reshape(depth, 1, -1).astype(F32)
    w_in = w['w_in']
    a3 = 3 * aw
    f_cols = jnp.pad(w_in[:, :, a3:a3 + n_heads], ((0, 0), (0, 0), (0, LANES - n_heads)))
    w_in_k = jnp.concatenate([w_in[:, :, :a3], w_in[:, :, a3 + n_heads:], f_cols], axis=2).astype(BF16)
    head_of_lane = np.arange(MXU_DIM) // HEAD_DIM
    bf16 = lambda name: w[name].astype(BF16)
    return dict(
        g_mix=row(w['g_mix']), w_in=w_in_k,
        b_f=jnp.pad(row(w['b_f']), ((0, 0), (0, 0), (0, LANES - n_heads))),
        g_qk=jnp.concatenate([row(w['g_q']), row(w['g_k'])], axis=2),
        bd=jnp.asarray(head_of_lane[:, None] == head_of_lane[None, :], BF16),
        w_glu=bf16('w_glu'), b_glu=row(w['b_glu']), g_attn_out=row(w['g_attn_out']), g_ssm_out=row(w['g_ssm_out']),
        w_out=bf16('w_out'), g_ffn=row(w['g_ffn']), w_ff1=bf16('w_ff1'), w_ff3=bf16('w_ff3'), w_ff2=bf16('w_ff2'),
        g_ple=row(w['g_ple']), w_pg=bf16('w_pg'), b_pg=row(w['b_pg']), w_pe=bf16('w_pe'),
    )


def kernel(x_prompt, x_sample, p_prompt, p_sample, cache_k, cache_v, cache_logf, state_ssm_re, state_ssm_im, page_table, g_mix, w_in, b_f, g_q, g_k, lam_re, lam_im, log_dt, b_re, b_im, c_re, c_im, d_skip, w_glu, b_glu, g_attn_out, g_ssm_out, w_out, g_ffn, w_ff1, w_ff3, w_ff2, w_pe, g_ple, w_pg, b_pg):
    w = dict(g_mix=g_mix, w_in=w_in, b_f=b_f, g_q=g_q, g_k=g_k, w_glu=w_glu, b_glu=b_glu,
             g_attn_out=g_attn_out, g_ssm_out=g_ssm_out, w_out=w_out, g_ffn=g_ffn, w_ff1=w_ff1, w_ff3=w_ff3,
             w_ff2=w_ff2, w_pe=w_pe, g_ple=g_ple, w_pg=w_pg, b_pg=b_pg)
    depth = w_in.shape[0]
    nb, seqlen, d = x_prompt.shape
    db, n_new, _ = x_sample.shape
    n_heads = b_f.shape[1]
    aw = n_heads * HEAD_DIM
    n_groups = log_dt.shape[1]
    page = cache_k.shape[2]
    ck = jnp.transpose(cache_k, (0, 1, 3, 4, 2))
    cv = jnp.transpose(cache_v, (0, 1, 3, 4, 2))
    clft = jnp.swapaxes(cache_logf, 2, 3)
    chunk_p = min(PROMPT_CHUNK, seqlen)

    pw = _prep_weights(n_heads, aw, w)
    blocks = _ssm_block_params(lam_re, lam_im, log_dt, b_re, b_im, c_re, c_im, d_skip, max(chunk_p, n_new))
    tables_p, scan_p = _ssm_tables(blocks, chunk_p), _ssm_scan_params(blocks, chunk_p)
    tables_s, scan_s = _ssm_tables(blocks, n_new), _ssm_scan_params(blocks, n_new)
    pp_all = p_prompt.reshape(depth, nb * seqlen, -1)
    ps_all = p_sample.reshape(depth, db * n_new, -1)

    h_p = x_prompt
    h_s = x_sample.reshape(1, db * n_new, d)
    outs = {k: [] for k in ('k_p', 'v_p', 'lf_p', 're_p', 'im_p', 'k_s', 'v_s', 'lf_s', 're_s', 'im_s')}
    for l in range(depth):
        q, k, v, kb, vb, u_c, lf, lft = _in_proj(h_p, pw, l, n_heads=n_heads, tm=ROW_TILE, chunk=chunk_p)
        negc = _negcumsum(lft.reshape(nb * n_heads, seqlen)).reshape(nb, n_heads, seqlen)
        attn = _attn_prompt(q, kb, vb, negc, n_heads=n_heads, tile=ATTN_TILE)
        y_c, x_last = _ssm(u_c, jnp.zeros((nb, n_groups * 2 * SSM_STATE), F32), tables_p, scan_p, l,
                           cb=SSM_CHUNKS_PER_STEP)
        h_p = _out_ffn(h_p.reshape(nb * seqlen, d), attn.reshape(nb * seqlen, aw),
                       y_c.reshape(nb * seqlen // chunk_p, -1), pp_all, pw, l,
                       tm=ROW_TILE, chunk=chunk_p).reshape(nb, seqlen, d)
        re, im = _unpack_state(x_last, n_groups)
        outs['k_p'].append(k.reshape(nb, seqlen, n_heads, HEAD_DIM))
        outs['v_p'].append(v.reshape(nb, seqlen, n_heads, HEAD_DIM))
        outs['lf_p'].append(jnp.swapaxes(lft, 1, 2))
        outs['re_p'].append(re)
        outs['im_p'].append(im)

        q, k, v, _, _, u, lf, _ = _in_proj(h_s, pw, l, n_heads=n_heads, tm=ROW_TILE, chunk=0)
        rs = lambda a: a.reshape(db, n_new, a.shape[-1])
        pad_page = lambda a: jnp.pad(rs(a), ((0, 0), (0, page - n_new), (0, 0)))
        lf_pad = jnp.pad(rs(lf), ((0, 0), (0, 0), (0, LANES - n_heads)))
        attn = _attn_sample(rs(q), pad_page(k), pad_page(v), lf_pad, ck, cv, clft, page_table, l,
                            n_heads=n_heads, pp=DEC_PAGES_PER_STEP)
        y_c, x_last = _ssm(_to_chunk_row(rs(u)), _pack_state(state_ssm_re[l], state_ssm_im[l]),
                           tables_s, scan_s, l, cb=1)
        h_s = _out_ffn(h_s.reshape(db * n_new, d), attn.reshape(db * n_new, aw),
                       _from_chunk_row(y_c, n_new).reshape(db * n_new, -1), ps_all, pw, l,
                       tm=ROW_TILE, chunk=0).reshape(1, db * n_new, d)
        re, im = _unpack_state(x_last, n_groups)
        outs['k_s'].append(k.reshape(db, n_new, n_heads, HEAD_DIM))
        outs['v_s'].append(v.reshape(db, n_new, n_heads, HEAD_DIM))
        outs['lf_s'].append(lf.reshape(db, n_new, n_heads))
        outs['re_s'].append(re)
        outs['im_s'].append(im)

    st = lambda name: jnp.stack(outs[name])
    return (h_p, h_s.reshape(db, n_new, d), st('k_p'), st('v_p'), st('lf_p'), st('re_p'), st('im_p'),
            st('k_s'), st('v_s'), st('lf_s'), st('re_s'), st('im_s'))
```

```python
import functools
import math

import numpy as np
import jax
import jax.numpy as jnp
from jax import lax
from jax.experimental import pallas as pl
from jax.experimental.pallas import tpu as pltpu

F32 = jnp.float32
BF16 = jnp.bfloat16

RMS_EPS = 1e-6
NEG_INF = -1e30
LOG2E = math.log2(math.e)
HEAD_DIM = 64
SSM_GROUP = 16
SSM_STATE = 64
LANES = 128
MXU_DIM = 256
VMEM_LIMIT_BYTES = 56 * 1024 * 1024
FUSED_VMEM_LIMIT_BYTES = 62 * 1024 * 1024

GROUPS_PER_BLOCK = LANES // SSM_GROUP
HALF_STATE = GROUPS_PER_BLOCK * SSM_STATE
BLOCK_STATE = 2 * HALF_STATE

PROMPT_CHUNK = 16
SSM_CHUNKS_PER_STEP = 32
ROW_TILE = 512
ATTN_TILE = 512
DIAG_SUB = 512
DEC_PAGES_PER_STEP = 8
FF_CHUNK = 256


def _params(semantics, vmem_limit_bytes=VMEM_LIMIT_BYTES):
    return pltpu.CompilerParams(dimension_semantics=semantics, vmem_limit_bytes=vmem_limit_bytes)


def _rms(x, g):
    return x * lax.rsqrt(jnp.mean(x * x, axis=-1, keepdims=True) + RMS_EPS) * g


def _sigmoid(x):
    return 1.0 / (1.0 + jnp.exp(-x))


def _log_sigmoid(x):
    return jnp.minimum(x, 0.0) - jnp.log1p(jnp.exp(-jnp.abs(x)))


def _gelu_tanh(x):
    c = math.sqrt(2.0 / math.pi)
    return 0.5 * x * (1.0 + jnp.tanh(c * (x + 0.044715 * (x * x * x))))


def _split3(x):
    h1 = x.astype(BF16)
    r1 = x - h1.astype(F32)
    h2 = r1.astype(BF16)
    h3 = (r1 - h2.astype(F32)).astype(BF16)
    return h1, h2, h3


def _dot(a, b):
    return jnp.dot(a, b, preferred_element_type=F32)


def _dot_nt(a, b):
    return lax.dot_general(a, b, (((1,), (1,)), ((), ())), preferred_element_type=F32)


def _dot3(x, w):
    h1, h2, h3 = _split3(x)
    return _dot(h1, w) + _dot(h2, w) + _dot(h3, w)


def _dot_nt_split(a, b):
    a1, a2, _ = _split3(a)
    b1, b2, _ = _split3(b)
    return _dot_nt(a1, b1) + _dot_nt(a1, b2) + _dot_nt(a2, b1)


def _const_spec(shape):
    nd = len(shape)
    return pl.BlockSpec(shape, lambda *_: (0,) * nd)


def _layer_spec(arr, layer, **kw):
    zeros = (0,) * (arr.ndim - 1)
    return pl.BlockSpec((None,) + arr.shape[1:], lambda *_: (layer,) + zeros, **kw)


def _rows_strided(start, size, stride):
    return pl.ds(start, size) if stride == 1 else pl.ds(start, size, stride=stride)


def _in_proj_kernel(x_ref, g_ref, w_ref, bf_ref, gqk_ref, bd_ref,
                    q_ref, k_ref, v_ref, kb_ref, vb_ref, u_ref, lf_ref, lft_ref, *scratch,
                    n_heads, aw, sw, chunk):
    n = _rms(x_ref[0], g_ref[...]).astype(BF16)
    proj = _dot(n, w_ref[...])
    bd = bd_ref[...]
    parts = []
    for c in range(2 * aw // MXU_DIM):
        blk = proj[:, c * MXU_DIM:(c + 1) * MXU_DIM]
        ssq = _dot((blk * blk).astype(BF16), bd)
        parts.append(blk * lax.rsqrt(ssq * (1.0 / HEAD_DIM) + RMS_EPS))
    qkn = jnp.concatenate(parts, axis=1) * gqk_ref[...]
    q_ref[0] = (qkn[:, :aw] * (LOG2E * HEAD_DIM ** -0.5)).astype(BF16)
    k = qkn[:, aw:]
    k_ref[0] = k
    kb_ref[0] = k.astype(BF16)
    v = proj[:, 2 * aw:3 * aw]
    v_ref[0] = v
    vb_ref[0] = v.astype(BF16)
    u = proj[:, 3 * aw:3 * aw + sw]
    if chunk:
        u_scr, = scratch
        rows_c = u.shape[0] // chunk
        for j in range(sw // LANES):
            u_scr[j] = u[:, j * LANES:(j + 1) * LANES]
            for t in range(chunk):
                piece = u_scr[j, pl.ds(t, rows_c, stride=chunk), :]
                u_ref[0, :, (j * chunk + t) * LANES:(j * chunk + t + 1) * LANES] = piece.astype(BF16)
    else:
        u_ref[0] = u.astype(BF16)
    lf = _log_sigmoid(proj[:, 3 * aw + sw:] + bf_ref[...])
    lf_ref[0] = lf[:, :n_heads]
    lft_ref[0] = lf.T[:n_heads, :]


def _in_proj(x, pw, layer, *, n_heads, tm, chunk):
    nb, seqlen, d = x.shape
    aw = n_heads * HEAD_DIM
    sw = pw['w_in'].shape[2] - 3 * aw - LANES
    tm = min(tm, seqlen)
    grid = (nb, seqlen // tm)
    row = lambda w: pl.BlockSpec((1, tm, w), lambda b, i: (b, i, 0))
    if chunk:
        u_shape = jax.ShapeDtypeStruct((nb, seqlen // chunk, chunk * sw), BF16)
        u_spec = pl.BlockSpec((1, tm // chunk, chunk * sw), lambda b, i: (b, i, 0))
        scratch = [pltpu.VMEM((sw // LANES, tm, LANES), F32)]
    else:
        u_shape = jax.ShapeDtypeStruct((nb, seqlen, sw), BF16)
        u_spec = row(sw)
        scratch = []
    out_shape = (
        jax.ShapeDtypeStruct((nb, seqlen, aw), BF16),
        jax.ShapeDtypeStruct((nb, seqlen, aw), F32),
        jax.ShapeDtypeStruct((nb, seqlen, aw), F32),
        jax.ShapeDtypeStruct((nb, seqlen, aw), BF16),
        jax.ShapeDtypeStruct((nb, seqlen, aw), BF16),
        u_shape,
        jax.ShapeDtypeStruct((nb, seqlen, n_heads), F32),
        jax.ShapeDtypeStruct((nb, n_heads, seqlen), F32),
    )
    out_specs = (row(aw), row(aw), row(aw), row(aw), row(aw), u_spec,
                 pl.BlockSpec((1, tm, n_heads), lambda b, i: (b, i, 0)),
                 pl.BlockSpec((1, n_heads, tm), lambda b, i: (b, 0, i)))
    weights = [pw['g_mix'], pw['w_in'], pw['b_f'], pw['g_qk']]
    in_specs = [row(d)] + [_layer_spec(a, layer) for a in weights] + [_const_spec(pw['bd'].shape)]
    return pl.pallas_call(
        functools.partial(_in_proj_kernel, n_heads=n_heads, aw=aw, sw=sw, chunk=chunk),
        grid=grid, in_specs=in_specs, out_specs=out_specs, out_shape=out_shape, scratch_shapes=scratch,
        compiler_params=_params(("parallel", "parallel")), name="in_proj",
    )(x, *weights, pw['bd'])


def _negcumsum_kernel(x_ref, tri_ref, ones_ref, o_ref):
    rows, seqlen = x_ref.shape
    tri = tri_ref[...]
    ones = ones_ref[...]
    carry = jnp.zeros((rows, LANES), F32)
    for j in range(seqlen // LANES):
        sl = slice(j * LANES, (j + 1) * LANES)
        xc = x_ref[:, sl]
        o_ref[:, sl] = -LOG2E * (_dot3(xc, tri) + carry)
        carry = carry + _dot3(xc, ones)


def _negcumsum(x):
    tri = jnp.asarray(np.triu(np.ones((LANES, LANES), np.float32)), BF16)
    ones = jnp.ones((LANES, LANES), BF16)
    return pl.pallas_call(
        _negcumsum_kernel, grid=(1,),
        in_specs=[_const_spec(x.shape), _const_spec(tri.shape), _const_spec(ones.shape)],
        out_specs=_const_spec(x.shape), out_shape=jax.ShapeDtypeStruct(x.shape, F32),
        compiler_params=_params(("arbitrary",)), name="negcumsum",
    )(x, tri, ones)


def _attn_kernel(qi_ref, ki_ref, q_ref, k_ref, v_ref, nc_ref, o_ref, *scratch, n_heads, tq, tk):
    step = pl.program_id(1)
    qi = qi_ref[step]
    ki = ki_ref[step]
    lane = lax.broadcasted_iota(jnp.int32, (1, LANES), 1)
    zero = jnp.zeros((), BF16)
    m_scr, acc_scr = scratch[:n_heads], scratch[n_heads:]

    def own_lanes(h):
        return (lane < HEAD_DIM) if h % 2 == 0 else (lane >= HEAD_DIM)

    def denom_lane(h):
        return HEAD_DIM if h % 2 == 0 else 0

    @pl.when(ki == 0)
    def _():
        for h in range(n_heads):
            m_scr[h][...] = jnp.full(m_scr[h].shape, NEG_INF, F32)
            acc_scr[h][...] = jnp.zeros(acc_scr[h].shape, F32)

    def head_block(h, r0, nr, nk, masked):
        psl = slice((h // 2) * LANES, (h // 2 + 1) * LANES)
        rows = slice(r0, r0 + nr)
        own = own_lanes(h)
        qh = jnp.where(own, q_ref[0, rows, psl], zero)
        s = _dot_nt(qh, k_ref[0, :nk, psl]) + nc_ref[0, h:h + 1, :nk]
        if masked:
            rowi = lax.broadcasted_iota(jnp.int32, (nr, nk), 0) + r0
            coli = lax.broadcasted_iota(jnp.int32, (nr, nk), 1)
            s = jnp.where(coli <= rowi, s, NEG_INF)
        m_prev = m_scr[h][rows]
        m_new = jnp.maximum(m_prev, jnp.max(s, axis=1, keepdims=True))
        p = jnp.exp2(s - jnp.tile(m_new, (1, nk // LANES)))
        m_scr[h][rows] = m_new
        vh = jnp.where(own, v_ref[0, :nk, psl], jnp.where(lane == denom_lane(h), 1.0, 0.0).astype(BF16))
        acc_scr[h][rows] = acc_scr[h][rows] * jnp.exp2(m_prev - m_new) + _dot(p.astype(BF16), vh)

    @pl.when(ki < qi)
    def _():
        for h in range(n_heads):
            head_block(h, 0, tq, tk, False)

    @pl.when(ki == qi)
    def _():
        sub = min(tq, DIAG_SUB)
        for h in range(n_heads):
            for r0 in range(0, tq, sub):
                head_block(h, r0, sub, r0 + sub, True)
        for pair in range(n_heads // 2):
            halves = []
            for h in (2 * pair, 2 * pair + 1):
                acc = acc_scr[h][...]
                halves.append(acc * (1.0 / acc[:, denom_lane(h):denom_lane(h) + 1]))
            out = jnp.where(own_lanes(0), halves[0], halves[1])
            o_ref[0, :, pair * LANES:(pair + 1) * LANES] = out.astype(o_ref.dtype)


def _attn_prompt(q, kb, vb, negc, *, n_heads, tile):
    nb, seqlen, aw = q.shape
    t = min(tile, seqlen)
    nq = seqlen // t
    pairs = [(i, j) for i in range(nq) for j in range(i + 1)]
    qi_tab = jnp.asarray([p[0] for p in pairs], jnp.int32)
    ki_tab = jnp.asarray([p[1] for p in pairs], jnp.int32)
    grid_spec = pltpu.PrefetchScalarGridSpec(
        num_scalar_prefetch=2, grid=(nb, len(pairs)),
        in_specs=[
            pl.BlockSpec((1, t, aw), lambda b, s, qi, ki: (b, qi[s], 0)),
            pl.BlockSpec((1, t, aw), lambda b, s, qi, ki: (b, ki[s], 0)),
            pl.BlockSpec((1, t, aw), lambda b, s, qi, ki: (b, ki[s], 0)),
            pl.BlockSpec((1, n_heads, t), lambda b, s, qi, ki: (b, 0, ki[s])),
        ],
        out_specs=pl.BlockSpec((1, t, aw), lambda b, s, qi, ki: (b, qi[s], 0)),
        scratch_shapes=[pltpu.VMEM((t, LANES), F32)] * (2 * n_heads),
    )
    return pl.pallas_call(
        functools.partial(_attn_kernel, n_heads=n_heads, tq=t, tk=t),
        grid_spec=grid_spec, out_shape=jax.ShapeDtypeStruct((nb, seqlen, aw), BF16),
        compiler_params=_params(("parallel", "arbitrary")), name="attn",
    )(qi_tab, ki_tab, q, kb, vb, negc)


def _dec_group(step, n_steps, nj, pt_ref, q_ref, kn_ref, vn_ref, lfn_ref, tri_ref, ones_ref,
               ck_hbm, cv_hbm, clf_hbm, o_ref, kbuf, vbuf, lfbuf, sem, qbd_scr, carry_scr, m_scr, l_scr, acc_scr,
               *, pp, n_heads, n_new, layer, n_pages, first, last):
    b = step // nj
    j = lax.rem(step, nj)
    slot = lax.rem(step, 2)
    rows = n_heads * n_new
    aw = n_heads * HEAD_DIM
    page = kbuf.shape[-1]

    def page_copies(seq, grp, half, lookup):
        out = []
        for i in range(pp):
            pg = pt_ref[seq, n_pages - 1 - (grp * pp + i)] if lookup else 0
            out.append(pltpu.make_async_copy(ck_hbm.at[layer, pg], kbuf.at[half, i], sem.at[0, half]))
            out.append(pltpu.make_async_copy(cv_hbm.at[layer, pg], vbuf.at[half, i], sem.at[1, half]))
            out.append(pltpu.make_async_copy(clf_hbm.at[layer, pg], lfbuf.at[half, i], sem.at[2, half]))
        return out

    def start_next():
        nxt = step + 1
        for cp in page_copies(nxt // nj, lax.rem(nxt, nj), 1 - slot, True):
            cp.start()

    if first:
        @pl.when(step == 0)
        def _():
            for cp in page_copies(0, 0, 0, True):
                cp.start()
    if last:
        pl.when(step + 1 < n_steps)(start_next)
    else:
        start_next()

    for cp in page_copies(b, j, slot, False):
        cp.wait()

    def pad_page(x):
        return jnp.concatenate([x, jnp.zeros((page - n_new, aw), F32)], axis=0).astype(BF16)

    def expand(r):
        return jnp.broadcast_to(r[:, None, :], (n_heads, n_new, r.shape[-1])).reshape(rows, r.shape[-1])

    def page_t(buf, i):
        return buf[slot, i].reshape(aw, page).astype(BF16)

    def start_sequence():
        qt = jnp.concatenate([q_ref[b].astype(F32)] * n_heads, axis=0)
        rowi = lax.broadcasted_iota(jnp.int32, (rows, aw), 0)
        lani = lax.broadcasted_iota(jnp.int32, (rows, aw), 1)
        qbd = jnp.where((lani // HEAD_DIM) == (rowi // n_new), qt, 0.0)
        qbd_scr[...] = qbd.astype(BF16)
        lfn = lfn_ref[b]
        run = jnp.zeros((1, LANES), F32)
        crow = []
        for t in range(n_new):
            run = run + lfn[t:t + 1, :]
            crow.append(run)
        cnew = jnp.concatenate(crow + [jnp.zeros((LANES - n_new, LANES), F32)], axis=0)
        bias = expand(-LOG2E * cnew.T[:n_heads, :])
        s = _dot_nt(qbd_scr[...], pad_page(kn_ref[b])) + bias
        rown = lax.broadcasted_iota(jnp.int32, (rows, LANES), 0)
        coln = lax.broadcasted_iota(jnp.int32, (rows, LANES), 1)
        s = jnp.where(coln <= (rown % n_new), s, NEG_INF)
        m = jnp.max(s, axis=1, keepdims=True)
        p = jnp.exp2(s - m)
        m_scr[...] = jnp.broadcast_to(m, m_scr.shape)
        l_scr[...] = jnp.broadcast_to(jnp.sum(p, axis=1, keepdims=True), l_scr.shape)
        acc_scr[...] = _dot(p.astype(BF16), pad_page(vn_ref[b]))
        carry_scr[...] = jnp.zeros(carry_scr.shape, F32)

    if first:
        pl.when(j == 0)(start_sequence)

    x = lfbuf[slot].reshape(pp * n_heads, page)
    within = _dot3(x, tri_ref[...])
    tot = _dot3(x, ones_ref[...])
    carry = carry_scr[...]
    qbd = qbd_scr[...]
    biases = []
    for i in range(pp):
        sl = slice(i * n_heads, (i + 1) * n_heads)
        biases.append(expand(LOG2E * (within[sl] + carry)))
        carry = carry + tot[sl]
    carry_scr[...] = carry
    s = jnp.concatenate([_dot(qbd, page_t(kbuf, i)) + biases[i] for i in range(pp)], axis=1)
    m_prev = m_scr[...]
    m_new = jnp.maximum(m_prev, jnp.max(s, axis=1, keepdims=True))
    alpha = jnp.exp2(m_prev - m_new)
    p = jnp.exp2(s - jnp.tile(m_new, (1, pp * page // LANES)))
    l_scr[...] = alpha * l_scr[...] + jnp.sum(p, axis=1, keepdims=True)
    m_scr[...] = m_new
    pv = _dot_nt(p[:, :page].astype(BF16), page_t(vbuf, 0))
    for i in range(1, pp):
        pv = pv + _dot_nt(p[:, i * page:(i + 1) * page].astype(BF16), page_t(vbuf, i))
    acc_scr[...] = acc_scr[...] * jnp.tile(alpha, (1, aw // LANES)) + pv

    def end_sequence():
        o = acc_scr[...] * jnp.tile(1.0 / l_scr[...], (1, aw // LANES))
        lano = lax.broadcasted_iota(jnp.int32, (n_new, aw), 1)
        out = jnp.zeros((n_new, aw), F32)
        for h in range(n_heads):
            out = jnp.where((lano // HEAD_DIM) == h, o[h * n_new:(h + 1) * n_new, :], out)
        o_ref[b] = out.astype(o_ref.dtype)

    return (lambda: pl.when(j == nj - 1)(end_sequence)) if last else None


def _dec_operands(q, k_new, v_new, lf_new, cache_k, cache_v, cache_lft, page_table, *, n_heads, pp):
    nb, n_new, aw = q.shape
    page = cache_k.shape[-1]
    rows = n_heads * n_new
    tri = jnp.asarray(np.tril(np.ones((page, page), np.float32), -1), BF16)
    ones = jnp.ones((page, page), BF16)
    whole = lambda a: pl.BlockSpec(a.shape, lambda i, pt: (0,) * a.ndim)
    hbm = pl.BlockSpec(memory_space=pl.ANY)
    small = [q, k_new, v_new, lf_new, tri, ones]
    operands = small + [cache_k, cache_v, cache_lft]
    in_specs = [whole(a) for a in small] + [hbm, hbm, hbm]
    out_shape = jax.ShapeDtypeStruct((nb, n_new, aw), BF16)
    out_spec = pl.BlockSpec((nb, n_new, aw), lambda i, pt: (0, 0, 0))
    scratch = [pltpu.VMEM((2, pp, n_heads, HEAD_DIM, page), F32),
               pltpu.VMEM((2, pp, n_heads, HEAD_DIM, page), F32),
               pltpu.VMEM((2, pp, n_heads, page), F32),
               pltpu.SemaphoreType.DMA((3, 2)),
               pltpu.VMEM((rows, aw), BF16), pltpu.VMEM((n_heads, page), F32),
               pltpu.VMEM((rows, LANES), F32), pltpu.VMEM((rows, LANES), F32), pltpu.VMEM((rows, aw), F32)]
    return operands, in_specs, out_shape, out_spec, scratch


def _ssm_block_params(lam_re, lam_im, log_dt, b_re, b_im, c_re, c_im, d_skip, t_max):
    depth, g, p = lam_re.shape
    gpb = GROUPS_PER_BLOCK
    nj = g // gpb
    lam = lax.complex(lam_re.astype(F32), lam_im.astype(F32))
    lam_dt = lam * jnp.exp(log_dt.astype(F32))[..., None]
    lam_bar = jnp.exp(lam_dt)
    b_bar = ((lam_bar - 1.0) / lam)[..., None] * lax.complex(b_re.astype(F32), b_im.astype(F32))
    c = lax.complex(c_re.astype(F32), c_im.astype(F32))
    steps = jnp.arange(t_max + 1, dtype=F32).astype(jnp.complex64)
    pw = jnp.exp(lam_dt[:, :, None, :] * steps[None, None, :, None])
    pw = pw.reshape(depth, nj, gpb, t_max + 1, p).transpose(0, 1, 3, 2, 4).reshape(depth, nj, t_max + 1, gpb * p)
    eye = jnp.asarray(np.eye(gpb, dtype=bool))[None, None, :, None, :, None]

    def block_diag(a):
        a = jnp.where(eye, a[:, :, :, :, None, :], 0.0)
        return a.reshape(depth, nj, gpb * SSM_GROUP, gpb * p)

    bmat = block_diag(b_bar.transpose(0, 1, 3, 2).reshape(depth, nj, gpb, SSM_GROUP, p))
    cmat = block_diag(c.reshape(depth, nj, gpb, SSM_GROUP, p))
    ri = lambda z: (jnp.real(z), jnp.imag(z))
    dsk = d_skip.astype(F32).reshape(depth, 1, g * SSM_GROUP)
    return ri(bmat), ri(cmat), ri(pw), dsk


def _ssm_tables_kernel(bre_ref, bim_ref, cre_ref, cim_ref, pwr_ref, pwi_ref, mt_ref, bt_ref, ct_ref, *, t):
    bre, bim = bre_ref[...], bim_ref[...]
    cre, cim = cre_ref[...], cim_ref[...]
    half = lambda i: slice(i * LANES, (i + 1) * LANES)
    lo, hi = slice(0, HALF_STATE), slice(HALF_STATE, BLOCK_STATE)
    bd = []
    for e in range(t):
        pr, pi = pwr_ref[e:e + 1, :], pwi_ref[e:e + 1, :]
        xr = bre * pr - bim * pi
        xi = bre * pi + bim * pr
        s = t - 1 - e
        bt_ref[s // 2, half(s % 2), lo] = xr.astype(BF16)
        bt_ref[s // 2, half(s % 2), hi] = xi.astype(BF16)
        bd.append(_dot_nt_split(xr, cre) - _dot_nt_split(xi, cim))
        pr, pi = pwr_ref[e + 1:e + 2, :], pwi_ref[e + 1:e + 2, :]
        ct_ref[e // 2, half(e % 2), lo] = (cre * pr - cim * pi).astype(BF16)
        ct_ref[e // 2, half(e % 2), hi] = (-(cre * pi + cim * pr)).astype(BF16)
    zero = jnp.zeros((LANES, LANES), BF16)
    for d in range(t // 2):
        mt_ref[d, half(0), half(0)] = bd[2 * d].astype(BF16)
        mt_ref[d, half(0), half(1)] = bd[2 * d + 1].astype(BF16)
        mt_ref[d, half(1), half(0)] = bd[2 * d - 1].astype(BF16) if d else zero
        mt_ref[d, half(1), half(1)] = bd[2 * d].astype(BF16)


def _ssm_tables(block_params, t):
    (bre, bim), (cre, cim), (pwr, pwi), _ = block_params
    depth, nj = bre.shape[:2]
    nt = t // 2
    mat = pl.BlockSpec((None, None) + bre.shape[2:], lambda l, j: (l, j, 0, 0))
    pws = pl.BlockSpec((None, None) + pwr.shape[2:], lambda l, j: (l, j, 0, 0))
    tile = lambda a, b: pl.BlockSpec((None, None, nt, a, b), lambda l, j: (l, j, 0, 0, 0))
    shape = lambda a, b: jax.ShapeDtypeStruct((depth, nj, nt, a, b), BF16)
    return pl.pallas_call(
        functools.partial(_ssm_tables_kernel, t=t), grid=(depth, nj),
        in_specs=[mat, mat, mat, mat, pws, pws],
        out_specs=(tile(MXU_DIM, MXU_DIM), tile(MXU_DIM, BLOCK_STATE), tile(MXU_DIM, BLOCK_STATE)),
        out_shape=(shape(MXU_DIM, MXU_DIM), shape(MXU_DIM, BLOCK_STATE), shape(MXU_DIM, BLOCK_STATE)),
        compiler_params=_params(("parallel", "parallel")), name="ssm_tables",
    )(bre, bim, cre, cim, pwr, pwi)


def _ssm_scan_params(block_params, t):
    _, _, (pwr, pwi), dsk = block_params
    depth, nj = pwr.shape[:2]
    ar, ai = pwr[:, :, t], pwi[:, :, t]
    a1 = jnp.concatenate([ar, ar], axis=-1).reshape(depth, 1, nj * BLOCK_STATE)
    a2 = jnp.concatenate([-ai, ai], axis=-1).reshape(depth, 1, nj * BLOCK_STATE)
    dsk_c = jnp.broadcast_to(dsk.reshape(depth, nj, 1, LANES), (depth, nj, t, LANES)).reshape(depth, 1, nj * t * LANES)
    return a1, a2, dsk_c


def _ssm_kernel(u_ref, x0_ref, a1_ref, a2_ref, dsk_ref, bt_ref, mt_ref, ct_ref, y_ref, xl_ref,
                s_scr, xin_scr, x_scr, *, nb, cb, nt):
    rows = nb * cb
    n_lt = BLOCK_STATE // LANES
    u = u_ref[...].reshape(rows, nt * MXU_DIM)
    tile = lambda a: slice(a * MXU_DIM, (a + 1) * MXU_DIM)
    lt = lambda i: slice(i * LANES, (i + 1) * LANES)

    @pl.when(pl.program_id(1) == 0)
    def _():
        x_scr[...] = x0_ref[...]

    s = _dot(u[:, tile(0)], bt_ref[0])
    for b in range(1, nt):
        s = s + _dot(u[:, tile(b)], bt_ref[b])
    for i in range(n_lt):
        s_scr[i] = s[:, lt(i)]
    a1 = jnp.broadcast_to(a1_ref[...], (nb, BLOCK_STATE))
    a2 = jnp.broadcast_to(a2_ref[...], (nb, BLOCK_STATE))

    def step(c, x):
        sel = _rows_strided(c, nb, cb)
        for i in range(n_lt):
            xin_scr[i, sel, :] = x[:, lt(i)]
        swapped = jnp.concatenate([x[:, HALF_STATE:], x[:, :HALF_STATE]], axis=1)
        s_c = jnp.concatenate([s_scr[i, sel, :] for i in range(n_lt)], axis=1)
        return a1 * x + a2 * swapped + s_c

    x = lax.fori_loop(0, cb, step, x_scr[...])
    x_scr[...] = x
    xl_ref[...] = x
    xin = jnp.concatenate([xin_scr[i] for i in range(n_lt)], axis=1).astype(BF16)
    for a in range(nt):
        acc = _dot_nt(xin, ct_ref[a])
        for b in range(a + 1):
            acc = acc + _dot(u[:, tile(b)], mt_ref[a - b])
        acc = acc + dsk_ref[:, tile(a)] * u[:, tile(a)].astype(F32)
        if len(y_ref.shape) == 3:
            y_ref[:, :, tile(a)] = acc.reshape(nb, cb, MXU_DIM)
        else:
            y_ref[:, tile(a)] = acc


def _ssm(u_c, x0, tables, scan_params, layer, *, cb):
    mt, bt, ct = tables
    a1, a2, dsk = scan_params
    nj, nt = mt.shape[1], mt.shape[2]
    w = nt * MXU_DIM
    nb = u_c.shape[0]
    if u_c.ndim == 3:
        n_chunks = u_c.shape[1]
        cb = min(cb, n_chunks)
        io_spec = pl.BlockSpec((nb, cb, w), lambda j, r: (0, r, j))
    else:
        n_chunks = cb = 1
        io_spec = pl.BlockSpec((nb, w), lambda j, r: (0, j))
    state = pl.BlockSpec((nb, BLOCK_STATE), lambda j, r: (0, j))
    vec = lambda width: pl.BlockSpec((None, 1, width), lambda j, r: (layer, 0, j))
    tab = lambda a: pl.BlockSpec((None, None) + a.shape[2:], lambda j, r: (layer, j, 0, 0, 0))
    rows = nb * cb
    n_lt = BLOCK_STATE // LANES
    return pl.pallas_call(
        functools.partial(_ssm_kernel, nb=nb, cb=cb, nt=nt),
        grid=(nj, n_chunks // cb),
        in_specs=[io_spec, state, vec(BLOCK_STATE), vec(BLOCK_STATE), vec(w), tab(bt), tab(mt), tab(ct)],
        out_specs=(io_spec, state),
        out_shape=(jax.ShapeDtypeStruct(u_c.shape, F32),
                   jax.ShapeDtypeStruct((nb, nj * BLOCK_STATE), F32)),
        scratch_shapes=[pltpu.VMEM((n_lt, rows, LANES), F32), pltpu.VMEM((n_lt, rows, LANES), F32),
                        pltpu.VMEM((nb, BLOCK_STATE), F32)],
        compiler_params=_params(("parallel", "arbitrary")), name="ssm",
    )(u_c, x0, a1, a2, dsk, bt, mt, ct)


def _to_chunk_row(u):
    nb, t, w = u.shape
    return u.reshape(nb, t, w // LANES, LANES).transpose(0, 2, 1, 3).reshape(nb, t * w)


def _from_chunk_row(y, t):
    nb, tw = y.shape
    w = tw // t
    return y.reshape(nb, w // LANES, t, LANES).transpose(0, 2, 1, 3).reshape(nb, t, w)


def _pack_state(re, im):
    nb, g, p = re.shape
    nj = g // GROUPS_PER_BLOCK
    x = jnp.stack([re.astype(F32).reshape(nb, nj, HALF_STATE), im.astype(F32).reshape(nb, nj, HALF_STATE)], axis=2)
    return x.reshape(nb, nj * BLOCK_STATE)


def _unpack_state(x, n_groups):
    nb = x.shape[0]
    x = x.reshape(nb, n_groups // GROUPS_PER_BLOCK, 2, GROUPS_PER_BLOCK, SSM_STATE)
    return x[:, :, 0].reshape(nb, n_groups, SSM_STATE), x[:, :, 1].reshape(nb, n_groups, SSM_STATE)


FFN_WEIGHTS = ('w_glu', 'b_glu', 'g_attn_out', 'g_ssm_out', 'w_out', 'g_ffn', 'w_ff1', 'w_ff3', 'w_ff2', 'g_ple',
               'w_pg', 'b_pg', 'w_pe')
N_FFN_WEIGHTS = len(FFN_WEIGHTS)
N_DEC_OPERANDS = 9


def _out_ffn_kernel(*refs, ff_chunk, chunk, dec):
    if dec:
        pt_ref, refs = refs[0], refs[1:]
    h_ref, a_ref, y_ref, p_ref = refs[:4]
    (wglu_ref, bglu_ref, ga_ref, gs_ref, wo_ref, gffn_ref, w1_ref, w3_ref, w2_ref, gple_ref, wpg_ref, bpg_ref,
     wpe_ref) = refs[4:4 + N_FFN_WEIGHTS]
    refs = refs[4 + N_FFN_WEIGHTS:]
    if dec:
        dec_in, refs = refs[:N_DEC_OPERANDS], refs[N_DEC_OPERANDS:]
    o_ref, refs = refs[0], refs[1:]
    if dec:
        od_ref, refs = refs[0], refs[1:]
    if chunk:
        y_scr, refs = refs[0], refs[1:]
    d_ff = w1_ref.shape[1]
    n_ff = d_ff // ff_chunk

    hooks = {}
    if dec:
        per_step = dec['per_step']
        statics = {k: dec[k] for k in ('pp', 'n_heads', 'n_new', 'layer', 'n_pages')}
        base = pl.program_id(0) * per_step

        finish = []

        def side(k):
            fin = _dec_group(base + k, dec['total'], dec['nj'], pt_ref, *dec_in, od_ref, *refs, **statics,
                             first=k == 0, last=k == per_step - 1)
            if fin is not None:
                finish.append(fin)

        side(0)
        for k in range(1, per_step):
            hooks.setdefault(((k - 1) * n_ff) // max(per_step - 1, 1), []).append(k)

    if chunk:
        rows_c = y_ref.shape[0]
        for j in range(y_scr.shape[0]):
            for t in range(chunk):
                y_scr[j, pl.ds(t, rows_c, stride=chunk), :] = (
                    y_ref[:, (j * chunk + t) * LANES:(j * chunk + t + 1) * LANES])
        y = jnp.concatenate([y_scr[j] for j in range(y_scr.shape[0])], axis=1)
    else:
        y = y_ref[...]
    y = _gelu_tanh(y)
    ssm = y * _sigmoid(_dot(y.astype(BF16), wglu_ref[...]) + bglu_ref[...])
    na = _rms(a_ref[...].astype(F32), ga_ref[...]).astype(BF16)
    ns = _rms(ssm, gs_ref[...]).astype(BF16)
    aw = na.shape[1]
    h = h_ref[...] + _dot(na, wo_ref[:aw, :]) + _dot(ns, wo_ref[aw:, :])
    n2 = _rms(h, gffn_ref[...]).astype(BF16)
    ff = jnp.zeros(h.shape, F32)
    for c in range(n_ff):
        for k in hooks.get(c, ()):
            side(k)
        sl = slice(c * ff_chunk, (c + 1) * ff_chunk)
        a = _dot(n2, w1_ref[:, sl])
        b = _dot(n2, w3_ref[:, sl])
        ff = ff + _dot((a * _sigmoid(a) * b).astype(BF16), w2_ref[sl, :])
    h = h + ff
    n3 = _rms(h, gple_ref[...]).astype(BF16)
    gate = _sigmoid(_dot(n3, wpg_ref[...]) + bpg_ref[...])
    o_ref[...] = h + gate * _dot(p_ref[...].astype(BF16), wpe_ref[...])
    if dec:
        for fin in finish:
            fin()


def _out_ffn(h, attn, yssm, p_all, pw, layer, *, tm, chunk, dec_args=None):
    m, d = h.shape
    tm = min(tm, m)
    n_steps = m // tm
    sw = pw['w_glu'].shape[1]
    row = lambda w: pl.BlockSpec((tm, w), lambda i, *_: (i, 0))
    weights = [pw[k] for k in FFN_WEIGHTS]
    d_ff = pw['w_ff1'].shape[2]
    ff_chunk = FF_CHUNK if d_ff % FF_CHUNK == 0 else d_ff
    if chunk:
        y_spec = pl.BlockSpec((tm // chunk, chunk * sw), lambda i, *_: (i, 0))
        scratch = [pltpu.VMEM((sw // LANES, tm, LANES), F32)]
    else:
        y_spec = row(sw)
        scratch = []
    p_spec = pl.BlockSpec((None, tm, p_all.shape[2]), lambda i, *_: (layer, i, 0))
    in_specs = [row(d), row(attn.shape[1]), y_spec, p_spec] + [
        _layer_spec(a, layer, pipeline_mode=pl.Buffered(1)) for a in weights]
    operands = [h, attn, yssm, p_all, *weights]
    out_specs, out_shape = row(d), jax.ShapeDtypeStruct((m, d), F32)
    if dec_args is None:
        return pl.pallas_call(
            functools.partial(_out_ffn_kernel, ff_chunk=ff_chunk, chunk=chunk, dec=None),
            grid=(n_steps,), in_specs=in_specs, out_specs=out_specs, out_shape=out_shape, scratch_shapes=scratch,
            compiler_params=_params(("parallel",)), name="out_ffn",
        )(*operands)
    *dec_arrays, page_table, n_heads = dec_args
    n_pages = page_table.shape[1]
    nb_s, n_new = dec_arrays[0].shape[:2]
    pp = DEC_PAGES_PER_STEP
    while n_pages % pp or (nb_s * (n_pages // pp)) % n_steps or (n_pages // pp) % (nb_s * (n_pages // pp) // n_steps):
        pp //= 2
    nj = n_pages // pp
    total = nb_s * nj
    d_ops, d_specs, d_shape, d_spec, d_scratch = _dec_operands(*dec_arrays, page_table, n_heads=n_heads, pp=pp)
    dec = dict(per_step=total // n_steps, total=total, nj=nj, pp=pp, n_heads=n_heads, n_new=n_new, layer=layer,
               n_pages=n_pages)
    grid_spec = pltpu.PrefetchScalarGridSpec(
        num_scalar_prefetch=1, grid=(n_steps,), in_specs=in_specs + d_specs, out_specs=(out_specs, d_spec),
        scratch_shapes=scratch + d_scratch)
    return pl.pallas_call(
        functools.partial(_out_ffn_kernel, ff_chunk=ff_chunk, chunk=chunk, dec=dec),
        grid_spec=grid_spec, out_shape=(out_shape, d_shape),
        compiler_params=_params(("arbitrary",), FUSED_VMEM_LIMIT_BYTES), name="out_ffn_dec",
    )(page_table, *operands, *d_ops)


def _prep_weights(n_heads, aw, w):
    depth = w['w_in'].shape[0]
    row = lambda a: a.reshape(depth, 1, -1).astype(F32)
    w_in = w['w_in']
    a3 = 3 * aw
    f_cols = jnp.pad(w_in[:, :, a3:a3 + n_heads], ((0, 0), (0, 0), (0, LANES - n_heads)))
    w_in_k = jnp.concatenate([w_in[:, :, :a3], w_in[:, :, a3 + n_heads:], f_cols], axis=2).astype(BF16)
    head_of_lane = np.arange(MXU_DIM) // HEAD_DIM
    bf16 = lambda name: w[name].astype(BF16)
    return dict(
        g_mix=row(w['g_mix']), w_in=w_in_k,
        b_f=jnp.pad(row(w['b_f']), ((0, 0), (0, 0), (0, LANES - n_heads))),
        g_qk=jnp.concatenate([row(w['g_q']), row(w['g_k'])], axis=2),
        bd=jnp.asarray(head_of_lane[:, None] == head_of_lane[None, :], BF16),
        w_glu=bf16('w_glu'), b_glu=row(w['b_glu']), g_attn_out=row(w['g_attn_out']), g_ssm_out=row(w['g_ssm_out']),
        w_out=bf16('w_out'), g_ffn=row(w['g_ffn']), w_ff1=bf16('w_ff1'), w_ff3=bf16('w_ff3'), w_ff2=bf16('w_ff2'),
        g_ple=row(w['g_ple']), w_pg=bf16('w_pg'), b_pg=row(w['b_pg']), w_pe=bf16('w_pe'),
    )


def kernel(x_prompt, x_sample, p_prompt, p_sample, cache_k, cache_v, cache_logf, state_ssm_re, state_ssm_im, page_table, g_mix, w_in, b_f, g_q, g_k, lam_re, lam_im, log_dt, b_re, b_im, c_re, c_im, d_skip, w_glu, b_glu, g_attn_out, g_ssm_out, w_out, g_ffn, w_ff1, w_ff3, w_ff2, w_pe, g_ple, w_pg, b_pg):
    w = dict(g_mix=g_mix, w_in=w_in, b_f=b_f, g_q=g_q, g_k=g_k, w_glu=w_glu, b_glu=b_glu,
             g_attn_out=g_attn_out, g_ssm_out=g_ssm_out, w_out=w_out, g_ffn=g_ffn, w_ff1=w_ff1, w_ff3=w_ff3,
             w_ff2=w_ff2, w_pe=w_pe, g_ple=g_ple, w_pg=w_pg, b_pg=b_pg)
    depth = w_in.shape[0]
    nb, seqlen, d = x_prompt.shape
    db, n_new, _ = x_sample.shape
    n_heads = b_f.shape[1]
    aw = n_heads * HEAD_DIM
    n_groups = log_dt.shape[1]
    ck = jnp.transpose(cache_k, (0, 1, 3, 4, 2))
    cv = jnp.transpose(cache_v, (0, 1, 3, 4, 2))
    clft = jnp.swapaxes(cache_logf, 2, 3)
    chunk_p = min(PROMPT_CHUNK, seqlen)

    pw = _prep_weights(n_heads, aw, w)
    blocks = _ssm_block_params(lam_re, lam_im, log_dt, b_re, b_im, c_re, c_im, d_skip, max(chunk_p, n_new))
    tables_p, scan_p = _ssm_tables(blocks, chunk_p), _ssm_scan_params(blocks, chunk_p)
    tables_s, scan_s = _ssm_tables(blocks, n_new), _ssm_scan_params(blocks, n_new)
    pp_all = p_prompt.reshape(depth, nb * seqlen, -1)
    ps_all = p_sample.reshape(depth, db * n_new, -1)

    h_p = x_prompt
    h_s = x_sample.reshape(1, db * n_new, d)
    outs = {k: [] for k in ('k_p', 'v_p', 'lf_p', 're_p', 'im_p', 'k_s', 'v_s', 'lf_s', 're_s', 'im_s')}
    for l in range(depth):
        q, k, v, kb, vb, u_c, lf, lft = _in_proj(h_p, pw, l, n_heads=n_heads, tm=ROW_TILE, chunk=chunk_p)
        negc = _negcumsum(lft.reshape(nb * n_heads, seqlen)).reshape(nb, n_heads, seqlen)
        attn = _attn_prompt(q, kb, vb, negc, n_heads=n_heads, tile=ATTN_TILE)
        y_c, x_last = _ssm(u_c, jnp.zeros((nb, n_groups * 2 * SSM_STATE), F32), tables_p, scan_p, l,
                           cb=SSM_CHUNKS_PER_STEP)
        re, im = _unpack_state(x_last, n_groups)
        outs['k_p'].append(k.reshape(nb, seqlen, n_heads, HEAD_DIM))
        outs['v_p'].append(v.reshape(nb, seqlen, n_heads, HEAD_DIM))
        outs['lf_p'].append(jnp.swapaxes(lft, 1, 2))
        outs['re_p'].append(re)
        outs['im_p'].append(im)

        q, k, v, _, _, u, lf, _ = _in_proj(h_s, pw, l, n_heads=n_heads, tm=ROW_TILE, chunk=0)
        rs = lambda a: a.reshape(db, n_new, a.shape[-1])
        lf_pad = jnp.pad(rs(lf), ((0, 0), (0, 0), (0, LANES - n_heads)))
        h_p, attn_s = _out_ffn(h_p.reshape(nb * seqlen, d), attn.reshape(nb * seqlen, aw),
                               y_c.reshape(nb * seqlen // chunk_p, -1), pp_all, pw, l, tm=ROW_TILE, chunk=chunk_p,
                               dec_args=(rs(q), rs(k), rs(v), lf_pad, ck, cv, clft, page_table, n_heads))
        h_p = h_p.reshape(nb, seqlen, d)
        y_c, x_last = _ssm(_to_chunk_row(rs(u)), _pack_state(state_ssm_re[l], state_ssm_im[l]),
                           tables_s, scan_s, l, cb=1)
        h_s = _out_ffn(h_s.reshape(db * n_new, d), attn_s.reshape(db * n_new, aw),
                       _from_chunk_row(y_c, n_new).reshape(db * n_new, -1), ps_all, pw, l,
                       tm=ROW_TILE, chunk=0).reshape(1, db * n_new, d)
        re, im = _unpack_state(x_last, n_groups)
        outs['k_s'].append(k.reshape(db, n_new, n_heads, HEAD_DIM))
        outs['v_s'].append(v.reshape(db, n_new, n_heads, HEAD_DIM))
        outs['lf_s'].append(lf.reshape(db, n_new, n_heads))
        outs['re_s'].append(re)
        outs['im_s'].append(im)

    st = lambda name: jnp.stack(outs[name])
    return (h_p, h_s.reshape(db, n_new, d), st('k_p'), st('v_p'), st('lf_p'), st('re_p'), st('im_p'),
            st('k_s'), st('v_s'), st('lf_s'), st('re_s'), st('im_s'))
```

```python
import functools
import math

import numpy as np
import jax
import jax.numpy as jnp
from jax import lax
from jax.experimental import pallas as pl
from jax.experimental.pallas import tpu as pltpu

F32 = jnp.float32
BF16 = jnp.bfloat16

RMS_EPS = 1e-6
NEG_INF = -1e30
LOG2E = math.log2(math.e)
HEAD_DIM = 64
SSM_GROUP = 16
SSM_STATE = 64
LANES = 128
MXU_DIM = 256
VMEM_LIMIT_BYTES = 56 * 1024 * 1024
FUSED_VMEM_LIMIT_BYTES = 62 * 1024 * 1024

GROUPS_PER_BLOCK = LANES // SSM_GROUP
HALF_STATE = GROUPS_PER_BLOCK * SSM_STATE
BLOCK_STATE = 2 * HALF_STATE

PROMPT_CHUNK = 16
SSM_CHUNKS_PER_STEP = 32
ROW_TILE = 512
ATTN_TILE = 512
DIAG_SUB = 512
DEC_PAGES_PER_STEP = 8
DEC_SLOTS = 3
FF_CHUNK = 256


def _params(semantics, vmem_limit_bytes=VMEM_LIMIT_BYTES):
    return pltpu.CompilerParams(dimension_semantics=semantics, vmem_limit_bytes=vmem_limit_bytes)


def _rms(x, g):
    return x * lax.rsqrt(jnp.mean(x * x, axis=-1, keepdims=True) + RMS_EPS) * g


def _sigmoid(x):
    return 1.0 / (1.0 + jnp.exp(-x))


def _log_sigmoid(x):
    return jnp.minimum(x, 0.0) - jnp.log1p(jnp.exp(-jnp.abs(x)))


def _gelu_tanh(x):
    c = math.sqrt(2.0 / math.pi)
    return 0.5 * x * (1.0 + jnp.tanh(c * (x + 0.044715 * (x * x * x))))


def _split3(x):
    h1 = x.astype(BF16)
    r1 = x - h1.astype(F32)
    h2 = r1.astype(BF16)
    h3 = (r1 - h2.astype(F32)).astype(BF16)
    return h1, h2, h3


def _dot(a, b):
    return jnp.dot(a, b, preferred_element_type=F32)


def _dot_nt(a, b):
    return lax.dot_general(a, b, (((1,), (1,)), ((), ())), preferred_element_type=F32)


def _dot3(x, w):
    h1, h2, h3 = _split3(x)
    return _dot(h1, w) + _dot(h2, w) + _dot(h3, w)


def _dot_nt_split(a, b):
    a1, a2, _ = _split3(a)
    b1, b2, _ = _split3(b)
    return _dot_nt(a1, b1) + _dot_nt(a1, b2) + _dot_nt(a2, b1)


def _const_spec(shape):
    nd = len(shape)
    return pl.BlockSpec(shape, lambda *_: (0,) * nd)


def _layer_spec(arr, layer, **kw):
    zeros = (0,) * (arr.ndim - 1)
    return pl.BlockSpec((None,) + arr.shape[1:], lambda *_: (layer,) + zeros, **kw)


def _rows_strided(start, size, stride):
    return pl.ds(start, size) if stride == 1 else pl.ds(start, size, stride=stride)


def _in_proj_kernel(x_ref, g_ref, w_ref, bf_ref, gqk_ref, bd_ref,
                    q_ref, k_ref, v_ref, kb_ref, vb_ref, u_ref, lf_ref, lft_ref, *scratch,
                    n_heads, aw, sw, chunk):
    n = _rms(x_ref[0], g_ref[...]).astype(BF16)
    proj = _dot(n, w_ref[...])
    bd = bd_ref[...]
    parts = []
    for c in range(2 * aw // MXU_DIM):
        blk = proj[:, c * MXU_DIM:(c + 1) * MXU_DIM]
        ssq = _dot((blk * blk).astype(BF16), bd)
        parts.append(blk * lax.rsqrt(ssq * (1.0 / HEAD_DIM) + RMS_EPS))
    qkn = jnp.concatenate(parts, axis=1) * gqk_ref[...]
    q_ref[0] = (qkn[:, :aw] * (LOG2E * HEAD_DIM ** -0.5)).astype(BF16)
    k = qkn[:, aw:]
    k_ref[0] = k
    kb_ref[0] = k.astype(BF16)
    v = proj[:, 2 * aw:3 * aw]
    v_ref[0] = v
    vb_ref[0] = v.astype(BF16)
    u = proj[:, 3 * aw:3 * aw + sw]
    if chunk:
        u_scr, = scratch
        rows_c = u.shape[0] // chunk
        for j in range(sw // LANES):
            u_scr[j] = u[:, j * LANES:(j + 1) * LANES]
            for t in range(chunk):
                piece = u_scr[j, pl.ds(t, rows_c, stride=chunk), :]
                u_ref[0, :, (j * chunk + t) * LANES:(j * chunk + t + 1) * LANES] = piece.astype(BF16)
    else:
        u_ref[0] = u.astype(BF16)
    lf = _log_sigmoid(proj[:, 3 * aw + sw:] + bf_ref[...])
    lf_ref[0] = lf[:, :n_heads]
    lft_ref[0] = lf.T[:n_heads, :]


def _in_proj(x, pw, layer, *, n_heads, tm, chunk):
    nb, seqlen, d = x.shape
    aw = n_heads * HEAD_DIM
    sw = pw['w_in'].shape[2] - 3 * aw - LANES
    tm = min(tm, seqlen)
    grid = (nb, seqlen // tm)
    row = lambda w: pl.BlockSpec((1, tm, w), lambda b, i: (b, i, 0))
    if chunk:
        u_shape = jax.ShapeDtypeStruct((nb, seqlen // chunk, chunk * sw), BF16)
        u_spec = pl.BlockSpec((1, tm // chunk, chunk * sw), lambda b, i: (b, i, 0))
        scratch = [pltpu.VMEM((sw // LANES, tm, LANES), F32)]
    else:
        u_shape = jax.ShapeDtypeStruct((nb, seqlen, sw), BF16)
        u_spec = row(sw)
        scratch = []
    out_shape = (
        jax.ShapeDtypeStruct((nb, seqlen, aw), BF16),
        jax.ShapeDtypeStruct((nb, seqlen, aw), F32),
        jax.ShapeDtypeStruct((nb, seqlen, aw), F32),
        jax.ShapeDtypeStruct((nb, seqlen, aw), BF16),
        jax.ShapeDtypeStruct((nb, seqlen, aw), BF16),
        u_shape,
        jax.ShapeDtypeStruct((nb, seqlen, n_heads), F32),
        jax.ShapeDtypeStruct((nb, n_heads, seqlen), F32),
    )
    out_specs = (row(aw), row(aw), row(aw), row(aw), row(aw), u_spec,
                 pl.BlockSpec((1, tm, n_heads), lambda b, i: (b, i, 0)),
                 pl.BlockSpec((1, n_heads, tm), lambda b, i: (b, 0, i)))
    weights = [pw['g_mix'], pw['w_in'], pw['b_f'], pw['g_qk']]
    in_specs = [row(d)] + [_layer_spec(a, layer) for a in weights] + [_const_spec(pw['bd'].shape)]
    return pl.pallas_call(
        functools.partial(_in_proj_kernel, n_heads=n_heads, aw=aw, sw=sw, chunk=chunk),
        grid=grid, in_specs=in_specs, out_specs=out_specs, out_shape=out_shape, scratch_shapes=scratch,
        compiler_params=_params(("parallel", "parallel")), name="in_proj",
    )(x, *weights, pw['bd'])


def _negcumsum_kernel(x_ref, tri_ref, ones_ref, o_ref):
    rows, seqlen = x_ref.shape
    tri = tri_ref[...]
    ones = ones_ref[...]
    carry = jnp.zeros((rows, LANES), F32)
    for j in range(seqlen // LANES):
        sl = slice(j * LANES, (j + 1) * LANES)
        xc = x_ref[:, sl]
        o_ref[:, sl] = -LOG2E * (_dot3(xc, tri) + carry)
        carry = carry + _dot3(xc, ones)


def _negcumsum(x):
    tri = jnp.asarray(np.triu(np.ones((LANES, LANES), np.float32)), BF16)
    ones = jnp.ones((LANES, LANES), BF16)
    return pl.pallas_call(
        _negcumsum_kernel, grid=(1,),
        in_specs=[_const_spec(x.shape), _const_spec(tri.shape), _const_spec(ones.shape)],
        out_specs=_const_spec(x.shape), out_shape=jax.ShapeDtypeStruct(x.shape, F32),
        compiler_params=_params(("arbitrary",)), name="negcumsum",
    )(x, tri, ones)


def _attn_kernel(qi_ref, ki_ref, q_ref, k_ref, v_ref, nc_ref, o_ref, *scratch, n_heads, tq, tk):
    step = pl.program_id(1)
    qi = qi_ref[step]
    ki = ki_ref[step]
    lane = lax.broadcasted_iota(jnp.int32, (1, LANES), 1)
    zero = jnp.zeros((), BF16)
    m_scr, acc_scr = scratch[:n_heads], scratch[n_heads:]

    def own_lanes(h):
        return (lane < HEAD_DIM) if h % 2 == 0 else (lane >= HEAD_DIM)

    def denom_lane(h):
        return HEAD_DIM if h % 2 == 0 else 0

    @pl.when(ki == 0)
    def _():
        for h in range(n_heads):
            m_scr[h][...] = jnp.full(m_scr[h].shape, NEG_INF, F32)
            acc_scr[h][...] = jnp.zeros(acc_scr[h].shape, F32)

    def head_block(h, r0, nr, nk, masked):
        psl = slice((h // 2) * LANES, (h // 2 + 1) * LANES)
        rows = slice(r0, r0 + nr)
        own = own_lanes(h)
        qh = jnp.where(own, q_ref[0, rows, psl], zero)
        s = _dot_nt(qh, k_ref[0, :nk, psl]) + nc_ref[0, h:h + 1, :nk]
        if masked:
            rowi = lax.broadcasted_iota(jnp.int32, (nr, nk), 0) + r0
            coli = lax.broadcasted_iota(jnp.int32, (nr, nk), 1)
            s = jnp.where(coli <= rowi, s, NEG_INF)
        m_prev = m_scr[h][rows]
        m_new = jnp.maximum(m_prev, jnp.max(s, axis=1, keepdims=True))
        p = jnp.exp2(s - jnp.tile(m_new, (1, nk // LANES)))
        m_scr[h][rows] = m_new
        vh = jnp.where(own, v_ref[0, :nk, psl], jnp.where(lane == denom_lane(h), 1.0, 0.0).astype(BF16))
        acc_scr[h][rows] = acc_scr[h][rows] * jnp.exp2(m_prev - m_new) + _dot(p.astype(BF16), vh)

    @pl.when(ki < qi)
    def _():
        for h in range(n_heads):
            head_block(h, 0, tq, tk, False)

    @pl.when(ki == qi)
    def _():
        sub = min(tq, DIAG_SUB)
        for h in range(n_heads):
            for r0 in range(0, tq, sub):
                head_block(h, r0, sub, r0 + sub, True)
        for pair in range(n_heads // 2):
            halves = []
            for h in (2 * pair, 2 * pair + 1):
                acc = acc_scr[h][...]
                halves.append(acc * (1.0 / acc[:, denom_lane(h):denom_lane(h) + 1]))
            out = jnp.where(own_lanes(0), halves[0], halves[1])
            o_ref[0, :, pair * LANES:(pair + 1) * LANES] = out.astype(o_ref.dtype)


def _attn_prompt(q, kb, vb, negc, *, n_heads, tile):
    nb, seqlen, aw = q.shape
    t = min(tile, seqlen)
    nq = seqlen // t
    pairs = [(i, j) for i in range(nq) for j in range(i + 1)]
    qi_tab = jnp.asarray([p[0] for p in pairs], jnp.int32)
    ki_tab = jnp.asarray([p[1] for p in pairs], jnp.int32)
    grid_spec = pltpu.PrefetchScalarGridSpec(
        num_scalar_prefetch=2, grid=(nb, len(pairs)),
        in_specs=[
            pl.BlockSpec((1, t, aw), lambda b, s, qi, ki: (b, qi[s], 0)),
            pl.BlockSpec((1, t, aw), lambda b, s, qi, ki: (b, ki[s], 0)),
            pl.BlockSpec((1, t, aw), lambda b, s, qi, ki: (b, ki[s], 0)),
            pl.BlockSpec((1, n_heads, t), lambda b, s, qi, ki: (b, 0, ki[s])),
        ],
        out_specs=pl.BlockSpec((1, t, aw), lambda b, s, qi, ki: (b, qi[s], 0)),
        scratch_shapes=[pltpu.VMEM((t, LANES), F32)] * (2 * n_heads),
    )
    return pl.pallas_call(
        functools.partial(_attn_kernel, n_heads=n_heads, tq=t, tk=t),
        grid_spec=grid_spec, out_shape=jax.ShapeDtypeStruct((nb, seqlen, aw), BF16),
        compiler_params=_params(("parallel", "arbitrary")), name="attn",
    )(qi_tab, ki_tab, q, kb, vb, negc)


def _dec_group(step, n_steps, nj, pt_ref, q_ref, kn_ref, vn_ref, lfn_ref, tri_ref, ones_ref,
               ck_hbm, cv_hbm, clf_hbm, o_ref, kbuf, vbuf, lfbuf, sem, qbd_scr, carry_scr, m_scr, l_scr, acc_scr,
               *, pp, n_heads, n_new, layer, n_pages, first, last):
    b = step // nj
    j = lax.rem(step, nj)
    n_slots = kbuf.shape[0]
    slot = lax.rem(step, n_slots)
    rows = n_heads * n_new
    aw = n_heads * HEAD_DIM
    page = kbuf.shape[-1]

    def page_copies(seq, grp, half, lookup):
        out = []
        for i in range(pp):
            pg = pt_ref[seq, n_pages - 1 - (grp * pp + i)] if lookup else 0
            out.append(pltpu.make_async_copy(ck_hbm.at[layer, pg], kbuf.at[half, i], sem.at[0, half]))
            out.append(pltpu.make_async_copy(cv_hbm.at[layer, pg], vbuf.at[half, i], sem.at[1, half]))
            out.append(pltpu.make_async_copy(clf_hbm.at[layer, pg], lfbuf.at[half, i], sem.at[2, half]))
        return out

    def start_group(grp, half):
        for cp in page_copies(grp // nj, lax.rem(grp, nj), half, True):
            cp.start()

    ahead = n_slots - 1
    if first:
        @pl.when(step == 0)
        def _():
            for g in range(ahead):
                for cp in page_copies(g // nj, g % nj, g, True):
                    cp.start()

    @pl.when(step + ahead < n_steps)
    def _():
        start_group(step + ahead, lax.rem(step + ahead, n_slots))

    for cp in page_copies(b, j, slot, False):
        cp.wait()

    def pad_page(x):
        return jnp.concatenate([x, jnp.zeros((page - n_new, aw), F32)], axis=0).astype(BF16)

    def expand(r):
        return jnp.broadcast_to(r[:, None, :], (n_heads, n_new, r.shape[-1])).reshape(rows, r.shape[-1])

    def page_t(buf, i):
        return buf[slot, i].reshape(aw, page).astype(BF16)

    def start_sequence():
        qt = jnp.concatenate([q_ref[b].astype(F32)] * n_heads, axis=0)
        rowi = lax.broadcasted_iota(jnp.int32, (rows, aw), 0)
        lani = lax.broadcasted_iota(jnp.int32, (rows, aw), 1)
        qbd = jnp.where((lani // HEAD_DIM) == (rowi // n_new), qt, 0.0)
        qbd_scr[...] = qbd.astype(BF16)
        lfn = lfn_ref[b]
        run = jnp.zeros((1, LANES), F32)
        crow = []
        for t in range(n_new):
            run = run + lfn[t:t + 1, :]
            crow.append(run)
        cnew = jnp.concatenate(crow + [jnp.zeros((LANES - n_new, LANES), F32)], axis=0)
        bias = expand(-LOG2E * cnew.T[:n_heads, :])
        s = _dot_nt(qbd_scr[...], pad_page(kn_ref[b])) + bias
        rown = lax.broadcasted_iota(jnp.int32, (rows, LANES), 0)
        coln = lax.broadcasted_iota(jnp.int32, (rows, LANES), 1)
        s = jnp.where(coln <= (rown % n_new), s, NEG_INF)
        m = jnp.max(s, axis=1, keepdims=True)
        p = jnp.exp2(s - m)
        m_scr[...] = jnp.broadcast_to(m, m_scr.shape)
        l_scr[...] = jnp.broadcast_to(jnp.sum(p, axis=1, keepdims=True), l_scr.shape)
        acc_scr[...] = _dot(p.astype(BF16), pad_page(vn_ref[b]))
        carry_scr[...] = jnp.zeros(carry_scr.shape, F32)

    if first:
        pl.when(j == 0)(start_sequence)

    x = lfbuf[slot].reshape(pp * n_heads, page)
    within = _dot3(x, tri_ref[...])
    tot = _dot3(x, ones_ref[...])
    carry = carry_scr[...]
    qbd = qbd_scr[...]
    biases = []
    for i in range(pp):
        sl = slice(i * n_heads, (i + 1) * n_heads)
        biases.append(expand(LOG2E * (within[sl] + carry)))
        carry = carry + tot[sl]
    carry_scr[...] = carry
    s = jnp.concatenate([_dot(qbd, page_t(kbuf, i)) + biases[i] for i in range(pp)], axis=1)
    m_prev = m_scr[...]
    m_new = jnp.maximum(m_prev, jnp.max(s, axis=1, keepdims=True))
    alpha = jnp.exp2(m_prev - m_new)
    p = jnp.exp2(s - jnp.tile(m_new, (1, pp * page // LANES)))
    l_scr[...] = alpha * l_scr[...] + jnp.sum(p, axis=1, keepdims=True)
    m_scr[...] = m_new
    pv = _dot_nt(p[:, :page].astype(BF16), page_t(vbuf, 0))
    for i in range(1, pp):
        pv = pv + _dot_nt(p[:, i * page:(i + 1) * page].astype(BF16), page_t(vbuf, i))
    acc_scr[...] = acc_scr[...] * jnp.tile(alpha, (1, aw // LANES)) + pv

    def end_sequence():
        o = acc_scr[...] * jnp.tile(1.0 / l_scr[...], (1, aw // LANES))
        lano = lax.broadcasted_iota(jnp.int32, (n_new, aw), 1)
        out = jnp.zeros((n_new, aw), F32)
        for h in range(n_heads):
            out = jnp.where((lano // HEAD_DIM) == h, o[h * n_new:(h + 1) * n_new, :], out)
        o_ref[b] = out.astype(o_ref.dtype)

    return (lambda: pl.when(j == nj - 1)(end_sequence)) if last else None


def _dec_operands(q, k_new, v_new, lf_new, cache_k, cache_v, cache_lft, page_table, *, n_heads, pp):
    nb, n_new, aw = q.shape
    page = cache_k.shape[-1]
    rows = n_heads * n_new
    tri = jnp.asarray(np.tril(np.ones((page, page), np.float32), -1), BF16)
    ones = jnp.ones((page, page), BF16)
    whole = lambda a: pl.BlockSpec(a.shape, lambda i, pt: (0,) * a.ndim)
    hbm = pl.BlockSpec(memory_space=pl.ANY)
    small = [q, k_new, v_new, lf_new, tri, ones]
    operands = small + [cache_k, cache_v, cache_lft]
    in_specs = [whole(a) for a in small] + [hbm, hbm, hbm]
    out_shape = jax.ShapeDtypeStruct((nb, n_new, aw), BF16)
    out_spec = pl.BlockSpec((nb, n_new, aw), lambda i, pt: (0, 0, 0))
    scratch = [pltpu.VMEM((DEC_SLOTS, pp, n_heads, HEAD_DIM, page), F32),
               pltpu.VMEM((DEC_SLOTS, pp, n_heads, HEAD_DIM, page), F32),
               pltpu.VMEM((DEC_SLOTS, pp, n_heads, page), F32),
               pltpu.SemaphoreType.DMA((3, DEC_SLOTS)),
               pltpu.VMEM((rows, aw), BF16), pltpu.VMEM((n_heads, page), F32),
               pltpu.VMEM((rows, LANES), F32), pltpu.VMEM((rows, LANES), F32), pltpu.VMEM((rows, aw), F32)]
    return operands, in_specs, out_shape, out_spec, scratch


def _ssm_block_params(lam_re, lam_im, log_dt, b_re, b_im, c_re, c_im, d_skip, t_max):
    depth, g, p = lam_re.shape
    gpb = GROUPS_PER_BLOCK
    nj = g // gpb
    lam = lax.complex(lam_re.astype(F32), lam_im.astype(F32))
    lam_dt = lam * jnp.exp(log_dt.astype(F32))[..., None]
    lam_bar = jnp.exp(lam_dt)
    b_bar = ((lam_bar - 1.0) / lam)[..., None] * lax.complex(b_re.astype(F32), b_im.astype(F32))
    c = lax.complex(c_re.astype(F32), c_im.astype(F32))
    steps = jnp.arange(t_max + 1, dtype=F32).astype(jnp.complex64)
    pw = jnp.exp(lam_dt[:, :, None, :] * steps[None, None, :, None])
    pw = pw.reshape(depth, nj, gpb, t_max + 1, p).transpose(0, 1, 3, 2, 4).reshape(depth, nj, t_max + 1, gpb * p)
    eye = jnp.asarray(np.eye(gpb, dtype=bool))[None, None, :, None, :, None]

    def block_diag(a):
        a = jnp.where(eye, a[:, :, :, :, None, :], 0.0)
        return a.reshape(depth, nj, gpb * SSM_GROUP, gpb * p)

    bmat = block_diag(b_bar.transpose(0, 1, 3, 2).reshape(depth, nj, gpb, SSM_GROUP, p))
    cmat = block_diag(c.reshape(depth, nj, gpb, SSM_GROUP, p))
    ri = lambda z: (jnp.real(z), jnp.imag(z))
    dsk = d_skip.astype(F32).reshape(depth, 1, g * SSM_GROUP)
    return ri(bmat), ri(cmat), ri(pw), dsk


def _ssm_tables_kernel(bre_ref, bim_ref, cre_ref, cim_ref, pwr_ref, pwi_ref, mt_ref, bt_ref, ct_ref, *, t):
    bre, bim = bre_ref[...], bim_ref[...]
    cre, cim = cre_ref[...], cim_ref[...]
    half = lambda i: slice(i * LANES, (i + 1) * LANES)
    lo, hi = slice(0, HALF_STATE), slice(HALF_STATE, BLOCK_STATE)
    bd = []
    for e in range(t):
        pr, pi = pwr_ref[e:e + 1, :], pwi_ref[e:e + 1, :]
        xr = bre * pr - bim * pi
        xi = bre * pi + bim * pr
        s = t - 1 - e
        bt_ref[s // 2, half(s % 2), lo] = xr.astype(BF16)
        bt_ref[s // 2, half(s % 2), hi] = xi.astype(BF16)
        bd.append(_dot_nt_split(xr, cre) - _dot_nt_split(xi, cim))
        pr, pi = pwr_ref[e + 1:e + 2, :], pwi_ref[e + 1:e + 2, :]
        ct_ref[e // 2, half(e % 2), lo] = (cre * pr - cim * pi).astype(BF16)
        ct_ref[e // 2, half(e % 2), hi] = (-(cre * pi + cim * pr)).astype(BF16)
    zero = jnp.zeros((LANES, LANES), BF16)
    for d in range(t // 2):
        mt_ref[d, half(0), half(0)] = bd[2 * d].astype(BF16)
        mt_ref[d, half(0), half(1)] = bd[2 * d + 1].astype(BF16)
        mt_ref[d, half(1), half(0)] = bd[2 * d - 1].astype(BF16) if d else zero
        mt_ref[d, half(1), half(1)] = bd[2 * d].astype(BF16)


def _ssm_tables(block_params, t):
    (bre, bim), (cre, cim), (pwr, pwi), _ = block_params
    depth, nj = bre.shape[:2]
    nt = t // 2
    mat = pl.BlockSpec((None, None) + bre.shape[2:], lambda l, j: (l, j, 0, 0))
    pws = pl.BlockSpec((None, None) + pwr.shape[2:], lambda l, j: (l, j, 0, 0))
    tile = lambda a, b: pl.BlockSpec((None, None, nt, a, b), lambda l, j: (l, j, 0, 0, 0))
    shape = lambda a, b: jax.ShapeDtypeStruct((depth, nj, nt, a, b), BF16)
    return pl.pallas_call(
        functools.partial(_ssm_tables_kernel, t=t), grid=(depth, nj),
        in_specs=[mat, mat, mat, mat, pws, pws],
        out_specs=(tile(MXU_DIM, MXU_DIM), tile(MXU_DIM, BLOCK_STATE), tile(MXU_DIM, BLOCK_STATE)),
        out_shape=(shape(MXU_DIM, MXU_DIM), shape(MXU_DIM, BLOCK_STATE), shape(MXU_DIM, BLOCK_STATE)),
        compiler_params=_params(("parallel", "parallel")), name="ssm_tables",
    )(bre, bim, cre, cim, pwr, pwi)


def _ssm_scan_params(block_params, t):
    _, _, (pwr, pwi), dsk = block_params
    depth, nj = pwr.shape[:2]
    ar, ai = pwr[:, :, t], pwi[:, :, t]
    a1 = jnp.concatenate([ar, ar], axis=-1).reshape(depth, 1, nj * BLOCK_STATE)
    a2 = jnp.concatenate([-ai, ai], axis=-1).reshape(depth, 1, nj * BLOCK_STATE)
    dsk_c = jnp.broadcast_to(dsk.reshape(depth, nj, 1, LANES), (depth, nj, t, LANES)).reshape(depth, 1, nj * t * LANES)
    return a1, a2, dsk_c


def _ssm_kernel(u_ref, x0_ref, a1_ref, a2_ref, dsk_ref, bt_ref, mt_ref, ct_ref, y_ref, xl_ref,
                s_scr, xin_scr, x_scr, *, nb, cb, nt):
    rows = nb * cb
    n_lt = BLOCK_STATE // LANES
    u = u_ref[...].reshape(rows, nt * MXU_DIM)
    tile = lambda a: slice(a * MXU_DIM, (a + 1) * MXU_DIM)
    lt = lambda i: slice(i * LANES, (i + 1) * LANES)

    @pl.when(pl.program_id(1) == 0)
    def _():
        x_scr[...] = x0_ref[...]

    s = _dot(u[:, tile(0)], bt_ref[0])
    for b in range(1, nt):
        s = s + _dot(u[:, tile(b)], bt_ref[b])
    for i in range(n_lt):
        s_scr[i] = s[:, lt(i)]
    a1 = jnp.broadcast_to(a1_ref[...], (nb, BLOCK_STATE))
    a2 = jnp.broadcast_to(a2_ref[...], (nb, BLOCK_STATE))

    def step(c, x):
        sel = _rows_strided(c, nb, cb)
        for i in range(n_lt):
            xin_scr[i, sel, :] = x[:, lt(i)]
        swapped = jnp.concatenate([x[:, HALF_STATE:], x[:, :HALF_STATE]], axis=1)
        s_c = jnp.concatenate([s_scr[i, sel, :] for i in range(n_lt)], axis=1)
        return a1 * x + a2 * swapped + s_c

    x = lax.fori_loop(0, cb, step, x_scr[...])
    x_scr[...] = x
    xl_ref[...] = x
    xin = jnp.concatenate([xin_scr[i] for i in range(n_lt)], axis=1).astype(BF16)
    for a in range(nt):
        acc = _dot_nt(xin, ct_ref[a])
        for b in range(a + 1):
            acc = acc + _dot(u[:, tile(b)], mt_ref[a - b])
        acc = acc + dsk_ref[:, tile(a)] * u[:, tile(a)].astype(F32)
        if len(y_ref.shape) == 3:
            y_ref[:, :, tile(a)] = acc.reshape(nb, cb, MXU_DIM)
        else:
            y_ref[:, tile(a)] = acc


def _ssm(u_c, x0, tables, scan_params, layer, *, cb):
    mt, bt, ct = tables
    a1, a2, dsk = scan_params
    nj, nt = mt.shape[1], mt.shape[2]
    w = nt * MXU_DIM
    nb = u_c.shape[0]
    if u_c.ndim == 3:
        n_chunks = u_c.shape[1]
        cb = min(cb, n_chunks)
        io_spec = pl.BlockSpec((nb, cb, w), lambda j, r: (0, r, j))
    else:
        n_chunks = cb = 1
        io_spec = pl.BlockSpec((nb, w), lambda j, r: (0, j))
    state = pl.BlockSpec((nb, BLOCK_STATE), lambda j, r: (0, j))
    vec = lambda width: pl.BlockSpec((None, 1, width), lambda j, r: (layer, 0, j))
    tab = lambda a: pl.BlockSpec((None, None) + a.shape[2:], lambda j, r: (layer, j, 0, 0, 0))
    rows = nb * cb
    n_lt = BLOCK_STATE // LANES
    return pl.pallas_call(
        functools.partial(_ssm_kernel, nb=nb, cb=cb, nt=nt),
        grid=(nj, n_chunks // cb),
        in_specs=[io_spec, state, vec(BLOCK_STATE), vec(BLOCK_STATE), vec(w), tab(bt), tab(mt), tab(ct)],
        out_specs=(io_spec, state),
        out_shape=(jax.ShapeDtypeStruct(u_c.shape, F32),
                   jax.ShapeDtypeStruct((nb, nj * BLOCK_STATE), F32)),
        scratch_shapes=[pltpu.VMEM((n_lt, rows, LANES), F32), pltpu.VMEM((n_lt, rows, LANES), F32),
                        pltpu.VMEM((nb, BLOCK_STATE), F32)],
        compiler_params=_params(("parallel", "arbitrary")), name="ssm",
    )(u_c, x0, a1, a2, dsk, bt, mt, ct)


def _to_chunk_row(u):
    nb, t, w = u.shape
    return u.reshape(nb, t, w // LANES, LANES).transpose(0, 2, 1, 3).reshape(nb, t * w)


def _from_chunk_row(y, t):
    nb, tw = y.shape
    w = tw // t
    return y.reshape(nb, w // LANES, t, LANES).transpose(0, 2, 1, 3).reshape(nb, t, w)


def _pack_state(re, im):
    nb, g, p = re.shape
    nj = g // GROUPS_PER_BLOCK
    x = jnp.stack([re.astype(F32).reshape(nb, nj, HALF_STATE), im.astype(F32).reshape(nb, nj, HALF_STATE)], axis=2)
    return x.reshape(nb, nj * BLOCK_STATE)


def _unpack_state(x, n_groups):
    nb = x.shape[0]
    x = x.reshape(nb, n_groups // GROUPS_PER_BLOCK, 2, GROUPS_PER_BLOCK, SSM_STATE)
    return x[:, :, 0].reshape(nb, n_groups, SSM_STATE), x[:, :, 1].reshape(nb, n_groups, SSM_STATE)


FFN_WEIGHTS = ('w_glu', 'b_glu', 'g_attn_out', 'g_ssm_out', 'w_out', 'g_ffn', 'w_ff1', 'w_ff3', 'w_ff2', 'g_ple',
               'w_pg', 'b_pg', 'w_pe')
N_FFN_WEIGHTS = len(FFN_WEIGHTS)
N_DEC_OPERANDS = 9


def _out_ffn_kernel(*refs, ff_chunk, chunk, dec):
    if dec:
        pt_ref, refs = refs[0], refs[1:]
    h_ref, a_ref, y_ref, p_ref = refs[:4]
    (wglu_ref, bglu_ref, ga_ref, gs_ref, wo_ref, gffn_ref, w1_ref, w3_ref, w2_ref, gple_ref, wpg_ref, bpg_ref,
     wpe_ref) = refs[4:4 + N_FFN_WEIGHTS]
    refs = refs[4 + N_FFN_WEIGHTS:]
    if dec:
        dec_in, refs = refs[:N_DEC_OPERANDS], refs[N_DEC_OPERANDS:]
    o_ref, refs = refs[0], refs[1:]
    if dec:
        od_ref, refs = refs[0], refs[1:]
    if chunk:
        y_scr, refs = refs[0], refs[1:]
    d_ff = w1_ref.shape[1]
    n_ff = d_ff // ff_chunk

    hooks = {}
    if dec:
        per_step = dec['per_step']
        statics = {k: dec[k] for k in ('pp', 'n_heads', 'n_new', 'layer', 'n_pages')}
        base = pl.program_id(0) * per_step

        finish = []

        def side(k):
            fin = _dec_group(base + k, dec['total'], dec['nj'], pt_ref, *dec_in, od_ref, *refs, **statics,
                             first=k == 0, last=k == per_step - 1)
            if fin is not None:
                finish.append(fin)

        side(0)
        for k in range(1, per_step):
            hooks.setdefault(((k - 1) * n_ff) // max(per_step - 1, 1), []).append(k)

    if chunk:
        rows_c = y_ref.shape[0]
        for j in range(y_scr.shape[0]):
            for t in range(chunk):
                y_scr[j, pl.ds(t, rows_c, stride=chunk), :] = (
                    y_ref[:, (j * chunk + t) * LANES:(j * chunk + t + 1) * LANES])
        y = jnp.concatenate([y_scr[j] for j in range(y_scr.shape[0])], axis=1)
    else:
        y = y_ref[...]
    y = _gelu_tanh(y)
    ssm = y * _sigmoid(_dot(y.astype(BF16), wglu_ref[...]) + bglu_ref[...])
    na = _rms(a_ref[...].astype(F32), ga_ref[...]).astype(BF16)
    ns = _rms(ssm, gs_ref[...]).astype(BF16)
    aw = na.shape[1]
    h = h_ref[...] + _dot(na, wo_ref[:aw, :]) + _dot(ns, wo_ref[aw:, :])
    n2 = _rms(h, gffn_ref[...]).astype(BF16)
    ff = jnp.zeros(h.shape, F32)
    for c in range(n_ff):
        for k in hooks.get(c, ()):
            side(k)
        sl = slice(c * ff_chunk, (c + 1) * ff_chunk)
        a = _dot(n2, w1_ref[:, sl])
        b = _dot(n2, w3_ref[:, sl])
        ff = ff + _dot((a * _sigmoid(a) * b).astype(BF16), w2_ref[sl, :])
    h = h + ff
    n3 = _rms(h, gple_ref[...]).astype(BF16)
    gate = _sigmoid(_dot(n3, wpg_ref[...]) + bpg_ref[...])
    o_ref[...] = h + gate * _dot(p_ref[...].astype(BF16), wpe_ref[...])
    if dec:
        for fin in finish:
            fin()


def _out_ffn(h, attn, yssm, p_all, pw, layer, *, tm, chunk, dec_args=None):
    m, d = h.shape
    tm = min(tm, m)
    n_steps = m // tm
    sw = pw['w_glu'].shape[1]
    row = lambda w: pl.BlockSpec((tm, w), lambda i, *_: (i, 0))
    weights = [pw[k] for k in FFN_WEIGHTS]
    d_ff = pw['w_ff1'].shape[2]
    ff_chunk = FF_CHUNK if d_ff % FF_CHUNK == 0 else d_ff
    if chunk:
        y_spec = pl.BlockSpec((tm // chunk, chunk * sw), lambda i, *_: (i, 0))
        scratch = [pltpu.VMEM((sw // LANES, tm, LANES), F32)]
    else:
        y_spec = row(sw)
        scratch = []
    p_spec = pl.BlockSpec((None, tm, p_all.shape[2]), lambda i, *_: (layer, i, 0))
    in_specs = [row(d), row(attn.shape[1]), y_spec, p_spec] + [
        _layer_spec(a, layer, pipeline_mode=pl.Buffered(1)) for a in weights]
    operands = [h, attn, yssm, p_all, *weights]
    out_specs, out_shape = row(d), jax.ShapeDtypeStruct((m, d), F32)
    if dec_args is None:
        return pl.pallas_call(
            functools.partial(_out_ffn_kernel, ff_chunk=ff_chunk, chunk=chunk, dec=None),
            grid=(n_steps,), in_specs=in_specs, out_specs=out_specs, out_shape=out_shape, scratch_shapes=scratch,
            compiler_params=_params(("parallel",)), name="out_ffn",
        )(*operands)
    *dec_arrays, page_table, n_heads = dec_args
    n_pages = page_table.shape[1]
    nb_s, n_new = dec_arrays[0].shape[:2]
    pp = DEC_PAGES_PER_STEP
    while n_pages % pp or (nb_s * (n_pages // pp)) % n_steps or (n_pages // pp) % (nb_s * (n_pages // pp) // n_steps):
        pp //= 2
    nj = n_pages // pp
    total = nb_s * nj
    d_ops, d_specs, d_shape, d_spec, d_scratch = _dec_operands(*dec_arrays, page_table, n_heads=n_heads, pp=pp)
    dec = dict(per_step=total // n_steps, total=total, nj=nj, pp=pp, n_heads=n_heads, n_new=n_new, layer=layer,
               n_pages=n_pages)
    grid_spec = pltpu.PrefetchScalarGridSpec(
        num_scalar_prefetch=1, grid=(n_steps,), in_specs=in_specs + d_specs, out_specs=(out_specs, d_spec),
        scratch_shapes=scratch + d_scratch)
    return pl.pallas_call(
        functools.partial(_out_ffn_kernel, ff_chunk=ff_chunk, chunk=chunk, dec=dec),
        grid_spec=grid_spec, out_shape=(out_shape, d_shape),
        compiler_params=_params(("arbitrary",), FUSED_VMEM_LIMIT_BYTES), name="out_ffn_dec",
    )(page_table, *operands, *d_ops)


def _prep_weights(n_heads, aw, w):
    depth = w['w_in'].shape[0]
    row = lambda a: a.reshape(depth, 1, -1).astype(F32)
    w_in = w['w_in']
    a3 = 3 * aw
    f_cols = jnp.pad(w_in[:, :, a3:a3 + n_heads], ((0, 0), (0, 0), (0, LANES - n_heads)))
    w_in_k = jnp.concatenate([w_in[:, :, :a3], w_in[:, :, a3 + n_heads:], f_cols], axis=2).astype(BF16)
    head_of_lane = np.arange(MXU_DIM) // HEAD_DIM
    bf16 = lambda name: w[name].astype(BF16)
    return dict(
        g_mix=row(w['g_mix']), w_in=w_in_k,
        b_f=jnp.pad(row(w['b_f']), ((0, 0), (0, 0), (0, LANES - n_heads))),
        g_qk=jnp.concatenate([row(w['g_q']), row(w['g_k'])], axis=2),
        bd=jnp.asarray(head_of_lane[:, None] == head_of_lane[None, :], BF16),
        w_glu=bf16('w_glu'), b_glu=row(w['b_glu']), g_attn_out=row(w['g_attn_out']), g_ssm_out=row(w['g_ssm_out']),
        w_out=bf16('w_out'), g_ffn=row(w['g_ffn']), w_ff1=bf16('w_ff1'), w_ff3=bf16('w_ff3'), w_ff2=bf16('w_ff2'),
        g_ple=row(w['g_ple']), w_pg=bf16('w_pg'), b_pg=row(w['b_pg']), w_pe=bf16('w_pe'),
    )


def kernel(x_prompt, x_sample, p_prompt, p_sample, cache_k, cache_v, cache_logf, state_ssm_re, state_ssm_im, page_table, g_mix, w_in, b_f, g_q, g_k, lam_re, lam_im, log_dt, b_re, b_im, c_re, c_im, d_skip, w_glu, b_glu, g_attn_out, g_ssm_out, w_out, g_ffn, w_ff1, w_ff3, w_ff2, w_pe, g_ple, w_pg, b_pg):
    w = dict(g_mix=g_mix, w_in=w_in, b_f=b_f, g_q=g_q, g_k=g_k, w_glu=w_glu, b_glu=b_glu,
             g_attn_out=g_attn_out, g_ssm_out=g_ssm_out, w_out=w_out, g_ffn=g_ffn, w_ff1=w_ff1, w_ff3=w_ff3,
             w_ff2=w_ff2, w_pe=w_pe, g_ple=g_ple, w_pg=w_pg, b_pg=b_pg)
    depth = w_in.shape[0]
    nb, seqlen, d = x_prompt.shape
    db, n_new, _ = x_sample.shape
    n_heads = b_f.shape[1]
    aw = n_heads * HEAD_DIM
    n_groups = log_dt.shape[1]
    ck = jnp.transpose(cache_k, (0, 1, 3, 4, 2))
    cv = jnp.transpose(cache_v, (0, 1, 3, 4, 2))
    clft = jnp.swapaxes(cache_logf, 2, 3)
    chunk_p = min(PROMPT_CHUNK, seqlen)

    pw = _prep_weights(n_heads, aw, w)
    blocks = _ssm_block_params(lam_re, lam_im, log_dt, b_re, b_im, c_re, c_im, d_skip, max(chunk_p, n_new))
    tables_p, scan_p = _ssm_tables(blocks, chunk_p), _ssm_scan_params(blocks, chunk_p)
    tables_s, scan_s = _ssm_tables(blocks, n_new), _ssm_scan_params(blocks, n_new)
    pp_all = p_prompt.reshape(depth, nb * seqlen, -1)
    ps_all = p_sample.reshape(depth, db * n_new, -1)

    h_p = x_prompt
    h_s = x_sample.reshape(1, db * n_new, d)
    outs = {k: [] for k in ('k_p', 'v_p', 'lf_p', 're_p', 'im_p', 'k_s', 'v_s', 'lf_s', 're_s', 'im_s')}
    for l in range(depth):
        q, k, v, kb, vb, u_c, lf, lft = _in_proj(h_p, pw, l, n_heads=n_heads, tm=ROW_TILE, chunk=chunk_p)
        negc = _negcumsum(lft.reshape(nb * n_heads, seqlen)).reshape(nb, n_heads, seqlen)
        attn = _attn_prompt(q, kb, vb, negc, n_heads=n_heads, tile=ATTN_TILE)
        y_c, x_last = _ssm(u_c, jnp.zeros((nb, n_groups * 2 * SSM_STATE), F32), tables_p, scan_p, l,
                           cb=SSM_CHUNKS_PER_STEP)
        re, im = _unpack_state(x_last, n_groups)
        outs['k_p'].append(k.reshape(nb, seqlen, n_heads, HEAD_DIM))
        outs['v_p'].append(v.reshape(nb, seqlen, n_heads, HEAD_DIM))
        outs['lf_p'].append(jnp.swapaxes(lft, 1, 2))
        outs['re_p'].append(re)
        outs['im_p'].append(im)

        q, k, v, _, _, u, lf, _ = _in_proj(h_s, pw, l, n_heads=n_heads, tm=ROW_TILE, chunk=0)
        rs = lambda a: a.reshape(db, n_new, a.shape[-1])
        lf_pad = jnp.pad(rs(lf), ((0, 0), (0, 0), (0, LANES - n_heads)))
        h_p, attn_s = _out_ffn(h_p.reshape(nb * seqlen, d), attn.reshape(nb * seqlen, aw),
                               y_c.reshape(nb * seqlen // chunk_p, -1), pp_all, pw, l, tm=ROW_TILE, chunk=chunk_p,
                               dec_args=(rs(q), rs(k), rs(v), lf_pad, ck, cv, clft, page_table, n_heads))
        h_p = h_p.reshape(nb, seqlen, d)
        y_c, x_last = _ssm(_to_chunk_row(rs(u)), _pack_state(state_ssm_re[l], state_ssm_im[l]),
                           tables_s, scan_s, l, cb=1)
        h_s = _out_ffn(h_s.reshape(db * n_new, d), attn_s.reshape(db * n_new, aw),
                       _from_chunk_row(y_c, n_new).reshape(db * n_new, -1), ps_all, pw, l,
                       tm=ROW_TILE, chunk=0).reshape(1, db * n_new, d)
        re, im = _unpack_state(x_last, n_groups)
        outs['k_s'].append(k.reshape(db, n_new, n_heads, HEAD_DIM))
        outs['v_s'].append(v.reshape(db, n_new, n_heads, HEAD_DIM))
        outs['lf_s'].append(lf.reshape(db, n_new, n_heads))
        outs['re_s'].append(re)
        outs['im_s'].append(im)

    st = lambda name: jnp.stack(outs[name])
    return (h_p, h_s.reshape(db, n_new, d), st('k_p'), st('v_p'), st('lf_p'), st('re_p'), st('im_p'),
            st('k_s'), st('v_s'), st('lf_s'), st('re_s'), st('im_s'))
```

```python
import functools
import math

import numpy as np
import jax
import jax.numpy as jnp
from jax import lax
from jax.experimental import pallas as pl
from jax.experimental.pallas import tpu as pltpu

F32 = jnp.float32
BF16 = jnp.bfloat16

RMS_EPS = 1e-6
NEG_INF = -1e30
LOG2E = math.log2(math.e)
HEAD_DIM = 64
SSM_GROUP = 16
SSM_STATE = 64
LANES = 128
MXU_DIM = 256
VMEM_LIMIT_BYTES = 56 * 1024 * 1024
FUSED_VMEM_LIMIT_BYTES = 62 * 1024 * 1024

GROUPS_PER_BLOCK = LANES // SSM_GROUP
HALF_STATE = GROUPS_PER_BLOCK * SSM_STATE
BLOCK_STATE = 2 * HALF_STATE

PROMPT_CHUNK = 16
SSM_CHUNKS_PER_STEP = 32
ROW_TILE = 512
ATTN_TILE = 512
DEC_PAGES_PER_STEP = 8
DEC_SLOTS = 2
SCAN_UNROLL = 4
FF_CHUNK = 256


def _params(semantics, vmem_limit_bytes=VMEM_LIMIT_BYTES):
    return pltpu.CompilerParams(dimension_semantics=semantics, vmem_limit_bytes=vmem_limit_bytes)


def _rms(x, g):
    return x * lax.rsqrt(jnp.mean(x * x, axis=-1, keepdims=True) + RMS_EPS) * g


def _sigmoid(x):
    return 1.0 / (1.0 + jnp.exp(-x))


def _log_sigmoid(x):
    return jnp.minimum(x, 0.0) - jnp.log1p(jnp.exp(-jnp.abs(x)))


def _gelu_tanh(x):
    c = math.sqrt(2.0 / math.pi)
    return 0.5 * x * (1.0 + jnp.tanh(c * (x + 0.044715 * (x * x * x))))


def _split3(x):
    h1 = x.astype(BF16)
    r1 = x - h1.astype(F32)
    h2 = r1.astype(BF16)
    h3 = (r1 - h2.astype(F32)).astype(BF16)
    return h1, h2, h3


def _dot(a, b):
    return jnp.dot(a, b, preferred_element_type=F32)


def _dot_nt(a, b):
    return lax.dot_general(a, b, (((1,), (1,)), ((), ())), preferred_element_type=F32)


def _dot3(x, w):
    h1, h2, h3 = _split3(x)
    return _dot(h1, w) + _dot(h2, w) + _dot(h3, w)


def _dot_nt_split(a, b):
    a1, a2, _ = _split3(a)
    b1, b2, _ = _split3(b)
    return _dot_nt(a1, b1) + _dot_nt(a1, b2) + _dot_nt(a2, b1)


def _const_spec(shape):
    nd = len(shape)
    return pl.BlockSpec(shape, lambda *_: (0,) * nd)


def _layer_spec(arr, layer, **kw):
    zeros = (0,) * (arr.ndim - 1)
    return pl.BlockSpec((None,) + arr.shape[1:], lambda *_: (layer,) + zeros, **kw)


def _rows_strided(start, size, stride):
    return pl.ds(start, size) if stride == 1 else pl.ds(start, size, stride=stride)


def _in_proj_kernel(x_ref, g_ref, w_ref, bf_ref, gqk_ref, bd_ref,
                    q_ref, k_ref, v_ref, kb_ref, vb_ref, u_ref, lf_ref, lft_ref, *scratch,
                    n_heads, aw, sw, chunk):
    n = _rms(x_ref[0], g_ref[...]).astype(BF16)
    proj = _dot(n, w_ref[...])
    bd = bd_ref[...]
    parts = []
    for c in range(2 * aw // MXU_DIM):
        blk = proj[:, c * MXU_DIM:(c + 1) * MXU_DIM]
        ssq = _dot((blk * blk).astype(BF16), bd)
        parts.append(blk * lax.rsqrt(ssq * (1.0 / HEAD_DIM) + RMS_EPS))
    qkn = jnp.concatenate(parts, axis=1) * gqk_ref[...]
    q_ref[0] = (qkn[:, :aw] * (LOG2E * HEAD_DIM ** -0.5)).astype(BF16)
    k = qkn[:, aw:]
    k_ref[0] = k
    kb_ref[0] = k.astype(BF16)
    v = proj[:, 2 * aw:3 * aw]
    v_ref[0] = v
    vb_ref[0] = v.astype(BF16)
    u = proj[:, 3 * aw:3 * aw + sw]
    if chunk:
        u_scr, = scratch
        rows_c = u.shape[0] // chunk
        for j in range(sw // LANES):
            u_scr[j] = u[:, j * LANES:(j + 1) * LANES]
            for t in range(chunk):
                piece = u_scr[j, pl.ds(t, rows_c, stride=chunk), :]
                u_ref[0, :, (j * chunk + t) * LANES:(j * chunk + t + 1) * LANES] = piece.astype(BF16)
    else:
        u_ref[0] = u.astype(BF16)
    lf = _log_sigmoid(proj[:, 3 * aw + sw:] + bf_ref[...])
    lf_ref[0] = lf[:, :n_heads]
    lft_ref[0] = lf.T[:n_heads, :]


def _in_proj(x, pw, layer, *, n_heads, tm, chunk):
    nb, seqlen, d = x.shape
    aw = n_heads * HEAD_DIM
    sw = pw['w_in'].shape[2] - 3 * aw - LANES
    tm = min(tm, seqlen)
    grid = (nb, seqlen // tm)
    row = lambda w: pl.BlockSpec((1, tm, w), lambda b, i: (b, i, 0))
    if chunk:
        u_shape = jax.ShapeDtypeStruct((nb, seqlen // chunk, chunk * sw), BF16)
        u_spec = pl.BlockSpec((1, tm // chunk, chunk * sw), lambda b, i: (b, i, 0))
        scratch = [pltpu.VMEM((sw // LANES, tm, LANES), F32)]
    else:
        u_shape = jax.ShapeDtypeStruct((nb, seqlen, sw), BF16)
        u_spec = row(sw)
        scratch = []
    out_shape = (
        jax.ShapeDtypeStruct((nb, seqlen, aw), BF16),
        jax.ShapeDtypeStruct((nb, seqlen, aw), F32),
        jax.ShapeDtypeStruct((nb, seqlen, aw), F32),
        jax.ShapeDtypeStruct((nb, seqlen, aw), BF16),
        jax.ShapeDtypeStruct((nb, seqlen, aw), BF16),
        u_shape,
        jax.ShapeDtypeStruct((nb, seqlen, n_heads), F32),
        jax.ShapeDtypeStruct((nb, n_heads, seqlen), F32),
    )
    out_specs = (row(aw), row(aw), row(aw), row(aw), row(aw), u_spec,
                 pl.BlockSpec((1, tm, n_heads), lambda b, i: (b, i, 0)),
                 pl.BlockSpec((1, n_heads, tm), lambda b, i: (b, 0, i)))
    weights = [pw['g_mix'], pw['w_in'], pw['b_f'], pw['g_qk']]
    in_specs = [row(d)] + [_layer_spec(a, layer) for a in weights] + [_const_spec(pw['bd'].shape)]
    return pl.pallas_call(
        functools.partial(_in_proj_kernel, n_heads=n_heads, aw=aw, sw=sw, chunk=chunk),
        grid=grid, in_specs=in_specs, out_specs=out_specs, out_shape=out_shape, scratch_shapes=scratch,
        compiler_params=_params(("parallel", "parallel")), name="in_proj",
    )(x, *weights, pw['bd'])


def _negcumsum_kernel(x_ref, tri_ref, ones_ref, o_ref):
    rows, seqlen = x_ref.shape
    tri = tri_ref[...]
    ones = ones_ref[...]
    carry = jnp.zeros((rows, LANES), F32)
    for j in range(seqlen // LANES):
        sl = slice(j * LANES, (j + 1) * LANES)
        xc = x_ref[:, sl]
        o_ref[:, sl] = -LOG2E * (_dot3(xc, tri) + carry)
        carry = carry + _dot3(xc, ones)


def _negcumsum(x):
    tri = jnp.asarray(np.triu(np.ones((LANES, LANES), np.float32)), BF16)
    ones = jnp.ones((LANES, LANES), BF16)
    return pl.pallas_call(
        _negcumsum_kernel, grid=(1,),
        in_specs=[_const_spec(x.shape), _const_spec(tri.shape), _const_spec(ones.shape)],
        out_specs=_const_spec(x.shape), out_shape=jax.ShapeDtypeStruct(x.shape, F32),
        compiler_params=_params(("arbitrary",)), name="negcumsum",
    )(x, tri, ones)


def _attn_kernel(qi_ref, ki_ref, q_ref, k_ref, v_ref, nc_ref, o_ref, *scratch, n_heads, tq, tk):
    step = pl.program_id(1)
    qi = qi_ref[step]
    ki = ki_ref[step]
    lane = lax.broadcasted_iota(jnp.int32, (1, LANES), 1)
    zero = jnp.zeros((), BF16)
    m_scr, acc_scr = scratch[:n_heads], scratch[n_heads:]

    def own_lanes(h):
        return (lane < HEAD_DIM) if h % 2 == 0 else (lane >= HEAD_DIM)

    def denom_lane(h):
        return HEAD_DIM if h % 2 == 0 else 0

    @pl.when(ki == 0)
    def _():
        for h in range(n_heads):
            m_scr[h][...] = jnp.full(m_scr[h].shape, NEG_INF, F32)
            acc_scr[h][...] = jnp.zeros(acc_scr[h].shape, F32)

    def head_block(h, causal):
        psl = slice((h // 2) * LANES, (h // 2 + 1) * LANES)
        own = own_lanes(h)
        qh = jnp.where(own, q_ref[0, :, psl], zero)
        s = _dot_nt(qh, k_ref[0, :, psl]) + nc_ref[0, h:h + 1, :]
        if causal is not None:
            s = jnp.where(causal, s, NEG_INF)
        m_prev = m_scr[h][...]
        m_new = jnp.maximum(m_prev, jnp.max(s, axis=1, keepdims=True))
        p = jnp.exp2((s - jnp.tile(m_new, (1, tk // LANES))).astype(BF16))
        m_scr[h][...] = m_new
        vh = jnp.where(own, v_ref[0, :, psl], jnp.where(lane == denom_lane(h), 1.0, 0.0).astype(BF16))
        acc_scr[h][...] = acc_scr[h][...] * jnp.exp2(m_prev - m_new) + _dot(p, vh)

    @pl.when(ki < qi)
    def _():
        for h in range(n_heads):
            head_block(h, None)

    @pl.when(ki == qi)
    def _():
        causal = (lax.broadcasted_iota(jnp.int32, (tq, tk), 1) <= lax.broadcasted_iota(jnp.int32, (tq, tk), 0))
        for h in range(n_heads):
            head_block(h, causal)
        for pair in range(n_heads // 2):
            halves = []
            for h in (2 * pair, 2 * pair + 1):
                acc = acc_scr[h][...]
                halves.append(acc * (1.0 / acc[:, denom_lane(h):denom_lane(h) + 1]))
            out = jnp.where(own_lanes(0), halves[0], halves[1])
            o_ref[0, :, pair * LANES:(pair + 1) * LANES] = out.astype(o_ref.dtype)


def _attn_prompt(q, kb, vb, negc, *, n_heads, tile):
    nb, seqlen, aw = q.shape
    t = min(tile, seqlen)
    nq = seqlen // t
    pairs = [(i, j) for i in range(nq) for j in range(i + 1)]
    qi_tab = jnp.asarray([p[0] for p in pairs], jnp.int32)
    ki_tab = jnp.asarray([p[1] for p in pairs], jnp.int32)
    grid_spec = pltpu.PrefetchScalarGridSpec(
        num_scalar_prefetch=2, grid=(nb, len(pairs)),
        in_specs=[
            pl.BlockSpec((1, t, aw), lambda b, s, qi, ki: (b, qi[s], 0)),
            pl.BlockSpec((1, t, aw), lambda b, s, qi, ki: (b, ki[s], 0)),
            pl.BlockSpec((1, t, aw), lambda b, s, qi, ki: (b, ki[s], 0)),
            pl.BlockSpec((1, n_heads, t), lambda b, s, qi, ki: (b, 0, ki[s])),
        ],
        out_specs=pl.BlockSpec((1, t, aw), lambda b, s, qi, ki: (b, qi[s], 0)),
        scratch_shapes=[pltpu.VMEM((t, LANES), F32)] * (2 * n_heads),
    )
    return pl.pallas_call(
        functools.partial(_attn_kernel, n_heads=n_heads, tq=t, tk=t),
        grid_spec=grid_spec, out_shape=jax.ShapeDtypeStruct((nb, seqlen, aw), BF16),
        compiler_params=_params(("parallel", "arbitrary")), name="attn",
    )(qi_tab, ki_tab, q, kb, vb, negc)


def _dec_group(step, n_steps, nj, pt_ref, q_ref, kn_ref, vn_ref, lfn_ref, tri_ref, ones_ref,
               ck_hbm, cv_hbm, clf_hbm, o_ref, kbuf, vbuf, lfbuf, sem, qbd_scr, carry_scr, m_scr, l_scr, acc_scr,
               *, pp, n_heads, n_new, layer, n_pages, first, last):
    b = step // nj
    j = lax.rem(step, nj)
    n_slots = kbuf.shape[0]
    slot = lax.rem(step, n_slots)
    rows = n_heads * n_new
    aw = n_heads * HEAD_DIM
    page = kbuf.shape[-1]

    def page_copies(seq, grp, half, lookup):
        out = []
        for i in range(pp):
            pg = pt_ref[seq, n_pages - 1 - (grp * pp + i)] if lookup else 0
            out.append(pltpu.make_async_copy(ck_hbm.at[layer, pg], kbuf.at[half, i], sem.at[0, half]))
            out.append(pltpu.make_async_copy(cv_hbm.at[layer, pg], vbuf.at[half, i], sem.at[1, half]))
            out.append(pltpu.make_async_copy(clf_hbm.at[layer, pg], lfbuf.at[half, i], sem.at[2, half]))
        return out

    def start_group(grp, half):
        for cp in page_copies(grp // nj, lax.rem(grp, nj), half, True):
            cp.start()

    ahead = n_slots - 1
    if first:
        @pl.when(step == 0)
        def _():
            for g in range(ahead):
                for cp in page_copies(g // nj, g % nj, g, True):
                    cp.start()

    @pl.when(step + ahead < n_steps)
    def _():
        start_group(step + ahead, lax.rem(step + ahead, n_slots))

    for cp in page_copies(b, j, slot, False):
        cp.wait()

    def pad_page(x):
        return jnp.concatenate([x, jnp.zeros((page - n_new, aw), F32)], axis=0).astype(BF16)

    def expand(r):
        return jnp.broadcast_to(r[:, None, :], (n_heads, n_new, r.shape[-1])).reshape(rows, r.shape[-1])

    def page_t(buf, i):
        return buf[slot, i].reshape(aw, page).astype(BF16)

    def start_sequence():
        qt = jnp.concatenate([q_ref[b].astype(F32)] * n_heads, axis=0)
        rowi = lax.broadcasted_iota(jnp.int32, (rows, aw), 0)
        lani = lax.broadcasted_iota(jnp.int32, (rows, aw), 1)
        qbd = jnp.where((lani // HEAD_DIM) == (rowi // n_new), qt, 0.0)
        qbd_scr[...] = qbd.astype(BF16)
        lfn = lfn_ref[b]
        run = jnp.zeros((1, LANES), F32)
        crow = []
        for t in range(n_new):
            run = run + lfn[t:t + 1, :]
            crow.append(run)
        cnew = jnp.concatenate(crow + [jnp.zeros((LANES - n_new, LANES), F32)], axis=0)
        bias = expand(-LOG2E * cnew.T[:n_heads, :])
        s = _dot_nt(qbd_scr[...], pad_page(kn_ref[b])) + bias
        rown = lax.broadcasted_iota(jnp.int32, (rows, LANES), 0)
        coln = lax.broadcasted_iota(jnp.int32, (rows, LANES), 1)
        s = jnp.where(coln <= (rown % n_new), s, NEG_INF)
        m = jnp.max(s, axis=1, keepdims=True)
        p = jnp.exp2(s - m)
        m_scr[...] = jnp.broadcast_to(m, m_scr.shape)
        l_scr[...] = jnp.broadcast_to(jnp.sum(p, axis=1, keepdims=True), l_scr.shape)
        acc_scr[...] = _dot(p.astype(BF16), pad_page(vn_ref[b]))
        carry_scr[...] = jnp.zeros(carry_scr.shape, F32)

    if first:
        pl.when(j == 0)(start_sequence)

    x = lfbuf[slot].reshape(pp * n_heads, page)
    within = _dot3(x, tri_ref[...])
    tot = _dot3(x, ones_ref[...])
    carry = carry_scr[...]
    qbd = qbd_scr[...]
    biases = []
    for i in range(pp):
        sl = slice(i * n_heads, (i + 1) * n_heads)
        biases.append(expand(LOG2E * (within[sl] + carry)))
        carry = carry + tot[sl]
    carry_scr[...] = carry
    s = jnp.concatenate([_dot(qbd, page_t(kbuf, i)) + biases[i] for i in range(pp)], axis=1)
    m_prev = m_scr[...]
    m_new = jnp.maximum(m_prev, jnp.max(s, axis=1, keepdims=True))
    alpha = jnp.exp2(m_prev - m_new)
    p = jnp.exp2(s - jnp.tile(m_new, (1, pp * page // LANES)))
    l_scr[...] = alpha * l_scr[...] + jnp.sum(p, axis=1, keepdims=True)
    m_scr[...] = m_new
    pv = _dot_nt(p[:, :page].astype(BF16), page_t(vbuf, 0))
    for i in range(1, pp):
        pv = pv + _dot_nt(p[:, i * page:(i + 1) * page].astype(BF16), page_t(vbuf, i))
    acc_scr[...] = acc_scr[...] * jnp.tile(alpha, (1, aw // LANES)) + pv

    def end_sequence():
        o = acc_scr[...] * jnp.tile(1.0 / l_scr[...], (1, aw // LANES))
        lano = lax.broadcasted_iota(jnp.int32, (n_new, aw), 1)
        out = jnp.zeros((n_new, aw), F32)
        for h in range(n_heads):
            out = jnp.where((lano // HEAD_DIM) == h, o[h * n_new:(h + 1) * n_new, :], out)
        o_ref[b] = out.astype(o_ref.dtype)

    return (lambda: pl.when(j == nj - 1)(end_sequence)) if last else None


def _dec_operands(q, k_new, v_new, lf_new, cache_k, cache_v, cache_lft, page_table, *, n_heads, pp):
    nb, n_new, aw = q.shape
    page = cache_k.shape[-1]
    rows = n_heads * n_new
    tri = jnp.asarray(np.tril(np.ones((page, page), np.float32), -1), BF16)
    ones = jnp.ones((page, page), BF16)
    whole = lambda a: pl.BlockSpec(a.shape, lambda i, pt: (0,) * a.ndim)
    hbm = pl.BlockSpec(memory_space=pl.ANY)
    small = [q, k_new, v_new, lf_new, tri, ones]
    operands = small + [cache_k, cache_v, cache_lft]
    in_specs = [whole(a) for a in small] + [hbm, hbm, hbm]
    out_shape = jax.ShapeDtypeStruct((nb, n_new, aw), BF16)
    out_spec = pl.BlockSpec((nb, n_new, aw), lambda i, pt: (0, 0, 0))
    scratch = [pltpu.VMEM((DEC_SLOTS, pp, n_heads, HEAD_DIM, page), F32),
               pltpu.VMEM((DEC_SLOTS, pp, n_heads, HEAD_DIM, page), F32),
               pltpu.VMEM((DEC_SLOTS, pp, n_heads, page), F32),
               pltpu.SemaphoreType.DMA((3, DEC_SLOTS)),
               pltpu.VMEM((rows, aw), BF16), pltpu.VMEM((n_heads, page), F32),
               pltpu.VMEM((rows, LANES), F32), pltpu.VMEM((rows, LANES), F32), pltpu.VMEM((rows, aw), F32)]
    return operands, in_specs, out_shape, out_spec, scratch


def _ssm_block_params(lam_re, lam_im, log_dt, b_re, b_im, c_re, c_im, d_skip, t_max):
    depth, g, p = lam_re.shape
    gpb = GROUPS_PER_BLOCK
    nj = g // gpb
    lam = lax.complex(lam_re.astype(F32), lam_im.astype(F32))
    lam_dt = lam * jnp.exp(log_dt.astype(F32))[..., None]
    lam_bar = jnp.exp(lam_dt)
    b_bar = ((lam_bar - 1.0) / lam)[..., None] * lax.complex(b_re.astype(F32), b_im.astype(F32))
    c = lax.complex(c_re.astype(F32), c_im.astype(F32))
    steps = jnp.arange(t_max + 1, dtype=F32).astype(jnp.complex64)
    pw = jnp.exp(lam_dt[:, :, None, :] * steps[None, None, :, None])
    pw = pw.reshape(depth, nj, gpb, t_max + 1, p).transpose(0, 1, 3, 2, 4).reshape(depth, nj, t_max + 1, gpb * p)
    eye = jnp.asarray(np.eye(gpb, dtype=bool))[None, None, :, None, :, None]

    def block_diag(a):
        a = jnp.where(eye, a[:, :, :, :, None, :], 0.0)
        return a.reshape(depth, nj, gpb * SSM_GROUP, gpb * p)

    bmat = block_diag(b_bar.transpose(0, 1, 3, 2).reshape(depth, nj, gpb, SSM_GROUP, p))
    cmat = block_diag(c.reshape(depth, nj, gpb, SSM_GROUP, p))
    ri = lambda z: (jnp.real(z), jnp.imag(z))
    dsk = d_skip.astype(F32).reshape(depth, 1, g * SSM_GROUP)
    return ri(bmat), ri(cmat), ri(pw), dsk


def _ssm_tables_kernel(bre_ref, bim_ref, cre_ref, cim_ref, pwr_ref, pwi_ref, mt_ref, bt_ref, ct_ref, *, t):
    bre, bim = bre_ref[...], bim_ref[...]
    cre, cim = cre_ref[...], cim_ref[...]
    half = lambda i: slice(i * LANES, (i + 1) * LANES)
    lo, hi = slice(0, HALF_STATE), slice(HALF_STATE, BLOCK_STATE)
    bd = []
    for e in range(t):
        pr, pi = pwr_ref[e:e + 1, :], pwi_ref[e:e + 1, :]
        xr = bre * pr - bim * pi
        xi = bre * pi + bim * pr
        s = t - 1 - e
        bt_ref[s // 2, half(s % 2), lo] = xr.astype(BF16)
        bt_ref[s // 2, half(s % 2), hi] = xi.astype(BF16)
        bd.append(_dot_nt_split(xr, cre) - _dot_nt_split(xi, cim))
        pr, pi = pwr_ref[e + 1:e + 2, :], pwi_ref[e + 1:e + 2, :]
        ct_ref[e // 2, half(e % 2), lo] = (cre * pr - cim * pi).astype(BF16)
        ct_ref[e // 2, half(e % 2), hi] = (-(cre * pi + cim * pr)).astype(BF16)
    zero = jnp.zeros((LANES, LANES), BF16)
    for d in range(t // 2):
        mt_ref[d, half(0), half(0)] = bd[2 * d].astype(BF16)
        mt_ref[d, half(0), half(1)] = bd[2 * d + 1].astype(BF16)
        mt_ref[d, half(1), half(0)] = bd[2 * d - 1].astype(BF16) if d else zero
        mt_ref[d, half(1), half(1)] = bd[2 * d].astype(BF16)


def _ssm_tables(block_params, t):
    (bre, bim), (cre, cim), (pwr, pwi), _ = block_params
    depth, nj = bre.shape[:2]
    nt = t // 2
    mat = pl.BlockSpec((None, None) + bre.shape[2:], lambda l, j: (l, j, 0, 0))
    pws = pl.BlockSpec((None, None) + pwr.shape[2:], lambda l, j: (l, j, 0, 0))
    tile = lambda a, b: pl.BlockSpec((None, None, nt, a, b), lambda l, j: (l, j, 0, 0, 0))
    shape = lambda a, b: jax.ShapeDtypeStruct((depth, nj, nt, a, b), BF16)
    return pl.pallas_call(
        functools.partial(_ssm_tables_kernel, t=t), grid=(depth, nj),
        in_specs=[mat, mat, mat, mat, pws, pws],
        out_specs=(tile(MXU_DIM, MXU_DIM), tile(MXU_DIM, BLOCK_STATE), tile(MXU_DIM, BLOCK_STATE)),
        out_shape=(shape(MXU_DIM, MXU_DIM), shape(MXU_DIM, BLOCK_STATE), shape(MXU_DIM, BLOCK_STATE)),
        compiler_params=_params(("parallel", "parallel")), name="ssm_tables",
    )(bre, bim, cre, cim, pwr, pwi)


def _ssm_scan_params(block_params, t):
    _, _, (pwr, pwi), dsk = block_params
    depth, nj = pwr.shape[:2]
    ar, ai = pwr[:, :, t], pwi[:, :, t]
    a1 = jnp.concatenate([ar, ar], axis=-1).reshape(depth, 1, nj * BLOCK_STATE)
    a2 = jnp.concatenate([-ai, ai], axis=-1).reshape(depth, 1, nj * BLOCK_STATE)
    dsk_c = jnp.broadcast_to(dsk.reshape(depth, nj, 1, LANES), (depth, nj, t, LANES)).reshape(depth, 1, nj * t * LANES)
    return a1, a2, dsk_c


def _ssm_kernel(u_ref, x0_ref, a1_ref, a2_ref, dsk_ref, bt_ref, mt_ref, ct_ref, y_ref, xl_ref,
                s_scr, xin_scr, x_scr, *, nb, cb, nt):
    rows = nb * cb
    n_lt = BLOCK_STATE // LANES
    u = u_ref[...].reshape(rows, nt * MXU_DIM)
    tile = lambda a: slice(a * MXU_DIM, (a + 1) * MXU_DIM)
    lt = lambda i: slice(i * LANES, (i + 1) * LANES)

    @pl.when(pl.program_id(1) == 0)
    def _():
        x_scr[...] = x0_ref[...]

    s = _dot(u[:, tile(0)], bt_ref[0])
    for b in range(1, nt):
        s = s + _dot(u[:, tile(b)], bt_ref[b])
    for i in range(n_lt):
        s_scr[i] = s[:, lt(i)]
    a1 = jnp.broadcast_to(a1_ref[...], (nb, BLOCK_STATE))
    a2 = jnp.broadcast_to(a2_ref[...], (nb, BLOCK_STATE))

    def step(c, x):
        sel = _rows_strided(c, nb, cb)
        for i in range(n_lt):
            xin_scr[i, sel, :] = x[:, lt(i)]
        swapped = jnp.concatenate([x[:, HALF_STATE:], x[:, :HALF_STATE]], axis=1)
        s_c = jnp.concatenate([s_scr[i, sel, :] for i in range(n_lt)], axis=1)
        return a1 * x + a2 * swapped + s_c

    x = lax.fori_loop(0, cb, step, x_scr[...], unroll=min(cb, SCAN_UNROLL))
    x_scr[...] = x
    xl_ref[...] = x
    xin = jnp.concatenate([xin_scr[i] for i in range(n_lt)], axis=1).astype(BF16)
    for a in range(nt):
        acc = _dot_nt(xin, ct_ref[a])
        for b in range(a + 1):
            acc = acc + _dot(u[:, tile(b)], mt_ref[a - b])
        acc = acc + dsk_ref[:, tile(a)] * u[:, tile(a)].astype(F32)
        if len(y_ref.shape) == 3:
            y_ref[:, :, tile(a)] = acc.reshape(nb, cb, MXU_DIM)
        else:
            y_ref[:, tile(a)] = acc


def _ssm(u_c, x0, tables, scan_params, layer, *, cb):
    mt, bt, ct = tables
    a1, a2, dsk = scan_params
    nj, nt = mt.shape[1], mt.shape[2]
    w = nt * MXU_DIM
    nb = u_c.shape[0]
    if u_c.ndim == 3:
        n_chunks = u_c.shape[1]
        cb = min(cb, n_chunks)
        io_spec = pl.BlockSpec((nb, cb, w), lambda j, r: (0, r, j))
    else:
        n_chunks = cb = 1
        io_spec = pl.BlockSpec((nb, w), lambda j, r: (0, j))
    state = pl.BlockSpec((nb, BLOCK_STATE), lambda j, r: (0, j))
    vec = lambda width: pl.BlockSpec((None, 1, width), lambda j, r: (layer, 0, j))
    tab = lambda a: pl.BlockSpec((None, None) + a.shape[2:], lambda j, r: (layer, j, 0, 0, 0))
    rows = nb * cb
    n_lt = BLOCK_STATE // LANES
    return pl.pallas_call(
        functools.partial(_ssm_kernel, nb=nb, cb=cb, nt=nt),
        grid=(nj, n_chunks // cb),
        in_specs=[io_spec, state, vec(BLOCK_STATE), vec(BLOCK_STATE), vec(w), tab(bt), tab(mt), tab(ct)],
        out_specs=(io_spec, state),
        out_shape=(jax.ShapeDtypeStruct(u_c.shape, F32),
                   jax.ShapeDtypeStruct((nb, nj * BLOCK_STATE), F32)),
        scratch_shapes=[pltpu.VMEM((n_lt, rows, LANES), F32), pltpu.VMEM((n_lt, rows, LANES), F32),
                        pltpu.VMEM((nb, BLOCK_STATE), F32)],
        compiler_params=_params(("parallel", "arbitrary")), name="ssm",
    )(u_c, x0, a1, a2, dsk, bt, mt, ct)


def _to_chunk_row(u):
    nb, t, w = u.shape
    return u.reshape(nb, t, w // LANES, LANES).transpose(0, 2, 1, 3).reshape(nb, t * w)


def _from_chunk_row(y, t):
    nb, tw = y.shape
    w = tw // t
    return y.reshape(nb, w // LANES, t, LANES).transpose(0, 2, 1, 3).reshape(nb, t, w)


def _pack_state(re, im):
    nb, g, p = re.shape
    nj = g // GROUPS_PER_BLOCK
    x = jnp.stack([re.astype(F32).reshape(nb, nj, HALF_STATE), im.astype(F32).reshape(nb, nj, HALF_STATE)], axis=2)
    return x.reshape(nb, nj * BLOCK_STATE)


def _unpack_state(x, n_groups):
    nb = x.shape[0]
    x = x.reshape(nb, n_groups // GROUPS_PER_BLOCK, 2, GROUPS_PER_BLOCK, SSM_STATE)
    return x[:, :, 0].reshape(nb, n_groups, SSM_STATE), x[:, :, 1].reshape(nb, n_groups, SSM_STATE)


FFN_WEIGHTS = ('w_glu', 'b_glu', 'g_attn_out', 'g_ssm_out', 'w_out', 'g_ffn', 'w_ff1', 'w_ff3', 'w_ff2', 'g_ple',
               'w_pg', 'b_pg', 'w_pe')
N_FFN_WEIGHTS = len(FFN_WEIGHTS)
N_DEC_OPERANDS = 9


def _out_ffn_kernel(*refs, ff_chunk, chunk, dec):
    if dec:
        pt_ref, refs = refs[0], refs[1:]
    h_ref, a_ref, y_ref, p_ref = refs[:4]
    (wglu_ref, bglu_ref, ga_ref, gs_ref, wo_ref, gffn_ref, w1_ref, w3_ref, w2_ref, gple_ref, wpg_ref, bpg_ref,
     wpe_ref) = refs[4:4 + N_FFN_WEIGHTS]
    refs = refs[4 + N_FFN_WEIGHTS:]
    if dec:
        dec_in, refs = refs[:N_DEC_OPERANDS], refs[N_DEC_OPERANDS:]
    o_ref, refs = refs[0], refs[1:]
    if dec:
        od_ref, refs = refs[0], refs[1:]
    if chunk:
        y_scr, refs = refs[0], refs[1:]
    d_ff = w1_ref.shape[1]
    n_ff = d_ff // ff_chunk

    hooks = {}
    if dec:
        per_step = dec['per_step']
        statics = {k: dec[k] for k in ('pp', 'n_heads', 'n_new', 'layer', 'n_pages')}
        base = pl.program_id(0) * per_step

        finish = []

        def side(k):
            fin = _dec_group(base + k, dec['total'], dec['nj'], pt_ref, *dec_in, od_ref, *refs, **statics,
                             first=k == 0, last=k == per_step - 1)
            if fin is not None:
                finish.append(fin)

        side(0)
        for k in range(1, per_step):
            hooks.setdefault(((k - 1) * n_ff) // max(per_step - 1, 1), []).append(k)

    if chunk:
        rows_c = y_ref.shape[0]
        for j in range(y_scr.shape[0]):
            for t in range(chunk):
                y_scr[j, pl.ds(t, rows_c, stride=chunk), :] = (
                    y_ref[:, (j * chunk + t) * LANES:(j * chunk + t + 1) * LANES])
        y = jnp.concatenate([y_scr[j] for j in range(y_scr.shape[0])], axis=1)
    else:
        y = y_ref[...]
    y = _gelu_tanh(y)
    ssm = y * _sigmoid(_dot(y.astype(BF16), wglu_ref[...]) + bglu_ref[...])
    na = _rms(a_ref[...].astype(F32), ga_ref[...]).astype(BF16)
    ns = _rms(ssm, gs_ref[...]).astype(BF16)
    aw = na.shape[1]
    h = h_ref[...] + _dot(na, wo_ref[:aw, :]) + _dot(ns, wo_ref[aw:, :])
    n2 = _rms(h, gffn_ref[...]).astype(BF16)
    ff = jnp.zeros(h.shape, F32)
    for c in range(n_ff):
        for k in hooks.get(c, ()):
            side(k)
        sl = slice(c * ff_chunk, (c + 1) * ff_chunk)
        a = _dot(n2, w1_ref[:, sl])
        b = _dot(n2, w3_ref[:, sl])
        ff = ff + _dot((a * _sigmoid(a) * b).astype(BF16), w2_ref[sl, :])
    h = h + ff
    n3 = _rms(h, gple_ref[...]).astype(BF16)
    gate = _sigmoid(_dot(n3, wpg_ref[...]) + bpg_ref[...])
    o_ref[...] = h + gate * _dot(p_ref[...].astype(BF16), wpe_ref[...])
    if dec:
        for fin in finish:
            fin()


def _out_ffn(h, attn, yssm, p_all, pw, layer, *, tm, chunk, dec_args=None):
    m, d = h.shape
    tm = min(tm, m)
    n_steps = m // tm
    sw = pw['w_glu'].shape[1]
    row = lambda w: pl.BlockSpec((tm, w), lambda i, *_: (i, 0))
    weights = [pw[k] for k in FFN_WEIGHTS]
    d_ff = pw['w_ff1'].shape[2]
    ff_chunk = FF_CHUNK if d_ff % FF_CHUNK == 0 else d_ff
    if chunk:
        y_spec = pl.BlockSpec((tm // chunk, chunk * sw), lambda i, *_: (i, 0))
        scratch = [pltpu.VMEM((sw // LANES, tm, LANES), F32)]
    else:
        y_spec = row(sw)
        scratch = []
    p_spec = pl.BlockSpec((None, tm, p_all.shape[2]), lambda i, *_: (layer, i, 0))
    in_specs = [row(d), row(attn.shape[1]), y_spec, p_spec] + [
        _layer_spec(a, layer, pipeline_mode=pl.Buffered(1)) for a in weights]
    operands = [h, attn, yssm, p_all, *weights]
    out_specs, out_shape = row(d), jax.ShapeDtypeStruct((m, d), F32)
    if dec_args is None:
        return pl.pallas_call(
            functools.partial(_out_ffn_kernel, ff_chunk=ff_chunk, chunk=chunk, dec=None),
            grid=(n_steps,), in_specs=in_specs, out_specs=out_specs, out_shape=out_shape, scratch_shapes=scratch,
            compiler_params=_params(("parallel",)), name="out_ffn",
        )(*operands)
    *dec_arrays, page_table, n_heads = dec_args
    n_pages = page_table.shape[1]
    nb_s, n_new = dec_arrays[0].shape[:2]
    pp = DEC_PAGES_PER_STEP
    while n_pages % pp or (nb_s * (n_pages // pp)) % n_steps or (n_pages // pp) % (nb_s * (n_pages // pp) // n_steps):
        pp //= 2
    nj = n_pages // pp
    total = nb_s * nj
    d_ops, d_specs, d_shape, d_spec, d_scratch = _dec_operands(*dec_arrays, page_table, n_heads=n_heads, pp=pp)
    dec = dict(per_step=total // n_steps, total=total, nj=nj, pp=pp, n_heads=n_heads, n_new=n_new, layer=layer,
               n_pages=n_pages)
    grid_spec = pltpu.PrefetchScalarGridSpec(
        num_scalar_prefetch=1, grid=(n_steps,), in_specs=in_specs + d_specs, out_specs=(out_specs, d_spec),
        scratch_shapes=scratch + d_scratch)
    return pl.pallas_call(
        functools.partial(_out_ffn_kernel, ff_chunk=ff_chunk, chunk=chunk, dec=dec),
        grid_spec=grid_spec, out_shape=(out_shape, d_shape),
        compiler_params=_params(("arbitrary",), FUSED_VMEM_LIMIT_BYTES), name="out_ffn_dec",
    )(page_table, *operands, *d_ops)


def _prep_weights(n_heads, aw, w):
    depth = w['w_in'].shape[0]
    row = lambda a: a.reshape(depth, 1, -1).astype(F32)
    w_in = w['w_in']
    a3 = 3 * aw
    f_cols = jnp.pad(w_in[:, :, a3:a3 + n_heads], ((0, 0), (0, 0), (0, LANES - n_heads)))
    w_in_k = jnp.concatenate([w_in[:, :, :a3], w_in[:, :, a3 + n_heads:], f_cols], axis=2).astype(BF16)
    head_of_lane = np.arange(MXU_DIM) // HEAD_DIM
    bf16 = lambda name: w[name].astype(BF16)
    return dict(
        g_mix=row(w['g_mix']), w_in=w_in_k,
        b_f=jnp.pad(row(w['b_f']), ((0, 0), (0, 0), (0, LANES - n_heads))),
        g_qk=jnp.concatenate([row(w['g_q']), row(w['g_k'])], axis=2),
        bd=jnp.asarray(head_of_lane[:, None] == head_of_lane[None, :], BF16),
        w_glu=bf16('w_glu'), b_glu=row(w['b_glu']), g_attn_out=row(w['g_attn_out']), g_ssm_out=row(w['g_ssm_out']),
        w_out=bf16('w_out'), g_ffn=row(w['g_ffn']), w_ff1=bf16('w_ff1'), w_ff3=bf16('w_ff3'), w_ff2=bf16('w_ff2'),
        g_ple=row(w['g_ple']), w_pg=bf16('w_pg'), b_pg=row(w['b_pg']), w_pe=bf16('w_pe'),
    )


def kernel(x_prompt, x_sample, p_prompt, p_sample, cache_k, cache_v, cache_logf, state_ssm_re, state_ssm_im, page_table, g_mix, w_in, b_f, g_q, g_k, lam_re, lam_im, log_dt, b_re, b_im, c_re, c_im, d_skip, w_glu, b_glu, g_attn_out, g_ssm_out, w_out, g_ffn, w_ff1, w_ff3, w_ff2, w_pe, g_ple, w_pg, b_pg):
    w = dict(g_mix=g_mix, w_in=w_in, b_f=b_f, g_q=g_q, g_k=g_k, w_glu=w_glu, b_glu=b_glu,
             g_attn_out=g_attn_out, g_ssm_out=g_ssm_out, w_out=w_out, g_ffn=g_ffn, w_ff1=w_ff1, w_ff3=w_ff3,
             w_ff2=w_ff2, w_pe=w_pe, g_ple=g_ple, w_pg=w_pg, b_pg=b_pg)
    depth = w_in.shape[0]
    nb, seqlen, d = x_prompt.shape
    db, n_new, _ = x_sample.shape
    n_heads = b_f.shape[1]
    aw = n_heads * HEAD_DIM
    n_groups = log_dt.shape[1]
    ck = jnp.transpose(cache_k, (0, 1, 3, 4, 2))
    cv = jnp.transpose(cache_v, (0, 1, 3, 4, 2))
    clft = jnp.swapaxes(cache_logf, 2, 3)
    chunk_p = min(PROMPT_CHUNK, seqlen)

    pw = _prep_weights(n_heads, aw, w)
    blocks = _ssm_block_params(lam_re, lam_im, log_dt, b_re, b_im, c_re, c_im, d_skip, max(chunk_p, n_new))
    tables_p, scan_p = _ssm_tables(blocks, chunk_p), _ssm_scan_params(blocks, chunk_p)
    tables_s, scan_s = _ssm_tables(blocks, n_new), _ssm_scan_params(blocks, n_new)
    pp_all = p_prompt.reshape(depth, nb * seqlen, -1)
    ps_all = p_sample.reshape(depth, db * n_new, -1)

    h_p = x_prompt
    h_s = x_sample.reshape(1, db * n_new, d)
    outs = {k: [] for k in ('k_p', 'v_p', 'lf_p', 're_p', 'im_p', 'k_s', 'v_s', 'lf_s', 're_s', 'im_s')}
    for l in range(depth):
        q, k, v, kb, vb, u_c, lf, lft = _in_proj(h_p, pw, l, n_heads=n_heads, tm=ROW_TILE, chunk=chunk_p)
        negc = _negcumsum(lft.reshape(nb * n_heads, seqlen)).reshape(nb, n_heads, seqlen)
        attn = _attn_prompt(q, kb, vb, negc, n_heads=n_heads, tile=ATTN_TILE)
        y_c, x_last = _ssm(u_c, jnp.zeros((nb, n_groups * 2 * SSM_STATE), F32), tables_p, scan_p, l,
                           cb=SSM_CHUNKS_PER_STEP)
        re, im = _unpack_state(x_last, n_groups)
        outs['k_p'].append(k.reshape(nb, seqlen, n_heads, HEAD_DIM))
        outs['v_p'].append(v.reshape(nb, seqlen, n_heads, HEAD_DIM))
        outs['lf_p'].append(jnp.swapaxes(lft, 1, 2))
        outs['re_p'].append(re)
        outs['im_p'].append(im)

        q, k, v, _, _, u, lf, _ = _in_proj(h_s, pw, l, n_heads=n_heads, tm=ROW_TILE, chunk=0)
        rs = lambda a: a.reshape(db, n_new, a.shape[-1])
        lf_pad = jnp.pad(rs(lf), ((0, 0), (0, 0), (0, LANES - n_heads)))
        h_p, attn_s = _out_ffn(h_p.reshape(nb * seqlen, d), attn.reshape(nb * seqlen, aw),
                               y_c.reshape(nb * seqlen // chunk_p, -1), pp_all, pw, l, tm=ROW_TILE, chunk=chunk_p,
                               dec_args=(rs(q), rs(k), rs(v), lf_pad, ck, cv, clft, page_table, n_heads))
        h_p = h_p.reshape(nb, seqlen, d)
        y_c, x_last = _ssm(_to_chunk_row(rs(u)), _pack_state(state_ssm_re[l], state_ssm_im[l]),
                           tables_s, scan_s, l, cb=1)
        h_s = _out_ffn(h_s.reshape(db * n_new, d), attn_s.reshape(db * n_new, aw),
                       _from_chunk_row(y_c, n_new).reshape(db * n_new, -1), ps_all, pw, l,
                       tm=ROW_TILE, chunk=0).reshape(1, db * n_new, d)
        re, im = _unpack_state(x_last, n_groups)
        outs['k_s'].append(k.reshape(db, n_new, n_heads, HEAD_DIM))
        outs['v_s'].append(v.reshape(db, n_new, n_heads, HEAD_DIM))
        outs['lf_s'].append(lf.reshape(db, n_new, n_heads))
        outs['re_s'].append(re)
        outs['im_s'].append(im)

    st = lambda name: jnp.stack(outs[name])
    return (h_p, h_s.reshape(db, n_new, d), st('k_p'), st('v_p'), st('lf_p'), st('re_p'), st('im_p'),
            st('k_s'), st('v_s'), st('lf_s'), st('re_s'), st('im_s'))
```

```python
import functools
import math

import numpy as np
import jax
import jax.numpy as jnp
from jax import lax
from jax.experimental import pallas as pl
from jax.experimental.pallas import tpu as pltpu

F32 = jnp.float32
BF16 = jnp.bfloat16

RMS_EPS = 1e-6
NEG_INF = -1e30
LOG2E = math.log2(math.e)
HEAD_DIM = 64
SSM_GROUP = 16
SSM_STATE = 64
LANES = 128
MXU_DIM = 256
VMEM_LIMIT_BYTES = 56 * 1024 * 1024
FUSED_VMEM_LIMIT_BYTES = 62 * 1024 * 1024

GROUPS_PER_BLOCK = LANES // SSM_GROUP
HALF_STATE = GROUPS_PER_BLOCK * SSM_STATE
BLOCK_STATE = 2 * HALF_STATE

PROMPT_CHUNK = 16
SSM_CHUNKS_PER_STEP = 32
ROW_TILE = 512
ATTN_TILE = 512
DEC_PAGES_PER_STEP = 8
DEC_SLOTS = 2
SCAN_UNROLL = 4
FF_CHUNK = 256


def _params(semantics, vmem_limit_bytes=VMEM_LIMIT_BYTES):
    return pltpu.CompilerParams(dimension_semantics=semantics, vmem_limit_bytes=vmem_limit_bytes)


def _rms(x, g):
    return x * lax.rsqrt(jnp.mean(x * x, axis=-1, keepdims=True) + RMS_EPS) * g


def _sigmoid(x):
    return 1.0 / (1.0 + jnp.exp(-x))


def _log_sigmoid(x):
    return jnp.minimum(x, 0.0) - jnp.log1p(jnp.exp(-jnp.abs(x)))


def _gelu_tanh(x):
    c = math.sqrt(2.0 / math.pi)
    return 0.5 * x * (1.0 + jnp.tanh(c * (x + 0.044715 * (x * x * x))))


def _split3(x):
    h1 = x.astype(BF16)
    r1 = x - h1.astype(F32)
    h2 = r1.astype(BF16)
    h3 = (r1 - h2.astype(F32)).astype(BF16)
    return h1, h2, h3


def _dot(a, b):
    return jnp.dot(a, b, preferred_element_type=F32)


def _dot_nt(a, b):
    return lax.dot_general(a, b, (((1,), (1,)), ((), ())), preferred_element_type=F32)


def _dot3(x, w):
    h1, h2, h3 = _split3(x)
    return _dot(h1, w) + _dot(h2, w) + _dot(h3, w)


def _dot_nt_split(a, b):
    a1, a2, _ = _split3(a)
    b1, b2, _ = _split3(b)
    return _dot_nt(a1, b1) + _dot_nt(a1, b2) + _dot_nt(a2, b1)


def _const_spec(shape):
    nd = len(shape)
    return pl.BlockSpec(shape, lambda *_: (0,) * nd)


def _layer_spec(arr, layer, **kw):
    zeros = (0,) * (arr.ndim - 1)
    return pl.BlockSpec((None,) + arr.shape[1:], lambda *_: (layer,) + zeros, **kw)


def _rows_strided(start, size, stride):
    return pl.ds(start, size) if stride == 1 else pl.ds(start, size, stride=stride)


def _in_proj_kernel(x_ref, g_ref, w_ref, bf_ref, gqk_ref, bd_ref,
                    q_ref, k_ref, v_ref, kb_ref, vb_ref, u_ref, lf_ref, lft_ref, *scratch,
                    n_heads, aw, sw, chunk):
    n = _rms(x_ref[0], g_ref[...]).astype(BF16)
    proj = _dot(n, w_ref[...])
    bd = bd_ref[...]
    parts = []
    for c in range(2 * aw // MXU_DIM):
        blk = proj[:, c * MXU_DIM:(c + 1) * MXU_DIM]
        ssq = _dot((blk * blk).astype(BF16), bd)
        parts.append(blk * lax.rsqrt(ssq * (1.0 / HEAD_DIM) + RMS_EPS))
    qkn = jnp.concatenate(parts, axis=1) * gqk_ref[...]
    q_ref[0] = (qkn[:, :aw] * (LOG2E * HEAD_DIM ** -0.5)).astype(BF16)
    k = qkn[:, aw:]
    k_ref[0] = k
    kb_ref[0] = k.astype(BF16)
    v = proj[:, 2 * aw:3 * aw]
    v_ref[0] = v
    vb_ref[0] = v.astype(BF16)
    u = proj[:, 3 * aw:3 * aw + sw]
    if chunk:
        u_scr, = scratch
        rows_c = u.shape[0] // chunk
        for j in range(sw // LANES):
            u_scr[j] = u[:, j * LANES:(j + 1) * LANES]
            for t in range(chunk):
                piece = u_scr[j, pl.ds(t, rows_c, stride=chunk), :]
                u_ref[0, :, (j * chunk + t) * LANES:(j * chunk + t + 1) * LANES] = piece.astype(BF16)
    else:
        u_ref[0] = u.astype(BF16)
    lf = _log_sigmoid(proj[:, 3 * aw + sw:] + bf_ref[...])
    lf_ref[0] = lf[:, :n_heads]
    lft_ref[0] = lf.T[:n_heads, :]


def _in_proj(x, pw, layer, *, n_heads, tm, chunk):
    nb, seqlen, d = x.shape
    aw = n_heads * HEAD_DIM
    sw = pw['w_in'].shape[2] - 3 * aw - LANES
    tm = min(tm, seqlen)
    grid = (nb, seqlen // tm)
    row = lambda w: pl.BlockSpec((1, tm, w), lambda b, i: (b, i, 0))
    if chunk:
        u_shape = jax.ShapeDtypeStruct((nb, seqlen // chunk, chunk * sw), BF16)
        u_spec = pl.BlockSpec((1, tm // chunk, chunk * sw), lambda b, i: (b, i, 0))
        scratch = [pltpu.VMEM((sw // LANES, tm, LANES), F32)]
    else:
        u_shape = jax.ShapeDtypeStruct((nb, seqlen, sw), BF16)
        u_spec = row(sw)
        scratch = []
    out_shape = (
        jax.ShapeDtypeStruct((nb, seqlen, aw), BF16),
        jax.ShapeDtypeStruct((nb, seqlen, aw), F32),
        jax.ShapeDtypeStruct((nb, seqlen, aw), F32),
        jax.ShapeDtypeStruct((nb, seqlen, aw), BF16),
        jax.ShapeDtypeStruct((nb, seqlen, aw), BF16),
        u_shape,
        jax.ShapeDtypeStruct((nb, seqlen, n_heads), F32),
        jax.ShapeDtypeStruct((nb, n_heads, seqlen), F32),
    )
    out_specs = (row(aw), row(aw), row(aw), row(aw), row(aw), u_spec,
                 pl.BlockSpec((1, tm, n_heads), lambda b, i: (b, i, 0)),
                 pl.BlockSpec((1, n_heads, tm), lambda b, i: (b, 0, i)))
    weights = [pw['g_mix'], pw['w_in'], pw['b_f'], pw['g_qk']]
    in_specs = [row(d)] + [_layer_spec(a, layer) for a in weights] + [_const_spec(pw['bd'].shape)]
    return pl.pallas_call(
        functools.partial(_in_proj_kernel, n_heads=n_heads, aw=aw, sw=sw, chunk=chunk),
        grid=grid, in_specs=in_specs, out_specs=out_specs, out_shape=out_shape, scratch_shapes=scratch,
        compiler_params=_params(("parallel", "parallel")), name="in_proj",
    )(x, *weights, pw['bd'])


def _negcumsum_kernel(x_ref, tri_ref, ones_ref, o_ref):
    rows, seqlen = x_ref.shape
    tri = tri_ref[...]
    ones = ones_ref[...]
    carry = jnp.zeros((rows, LANES), F32)
    for j in range(seqlen // LANES):
        sl = slice(j * LANES, (j + 1) * LANES)
        xc = x_ref[:, sl]
        o_ref[:, sl] = -LOG2E * (_dot3(xc, tri) + carry)
        carry = carry + _dot3(xc, ones)


def _negcumsum(x):
    tri = jnp.asarray(np.triu(np.ones((LANES, LANES), np.float32)), BF16)
    ones = jnp.ones((LANES, LANES), BF16)
    return pl.pallas_call(
        _negcumsum_kernel, grid=(1,),
        in_specs=[_const_spec(x.shape), _const_spec(tri.shape), _const_spec(ones.shape)],
        out_specs=_const_spec(x.shape), out_shape=jax.ShapeDtypeStruct(x.shape, F32),
        compiler_params=_params(("arbitrary",)), name="negcumsum",
    )(x, tri, ones)


def _attn_kernel(qi_ref, ki_ref, q_ref, k_ref, v_ref, nc_ref, o_ref, *scratch, n_heads, tq, tk):
    step = pl.program_id(1)
    qi = qi_ref[step]
    ki = ki_ref[step]
    lane = lax.broadcasted_iota(jnp.int32, (1, LANES), 1)
    zero = jnp.zeros((), BF16)
    m_scr, acc_scr = scratch[:n_heads], scratch[n_heads:]

    def own_lanes(h):
        return (lane < HEAD_DIM) if h % 2 == 0 else (lane >= HEAD_DIM)

    def denom_lane(h):
        return HEAD_DIM if h % 2 == 0 else 0

    @pl.when(ki == 0)
    def _():
        for h in range(n_heads):
            m_scr[h][...] = jnp.full(m_scr[h].shape, NEG_INF, F32)
            acc_scr[h][...] = jnp.zeros(acc_scr[h].shape, F32)

    def head_block(h, causal):
        psl = slice((h // 2) * LANES, (h // 2 + 1) * LANES)
        own = own_lanes(h)
        qh = jnp.where(own, q_ref[0, :, psl], zero)
        s = _dot_nt(qh, k_ref[0, :, psl]) + nc_ref[0, h:h + 1, :]
        if causal is not None:
            s = jnp.where(causal, s, NEG_INF)
        m_prev = m_scr[h][...]
        m_new = jnp.maximum(m_prev, jnp.max(s, axis=1, keepdims=True))
        p = jnp.exp2((s - jnp.tile(m_new, (1, tk // LANES))).astype(BF16))
        m_scr[h][...] = m_new
        vh = jnp.where(own, v_ref[0, :, psl], jnp.where(lane == denom_lane(h), 1.0, 0.0).astype(BF16))
        acc_scr[h][...] = acc_scr[h][...] * jnp.exp2(m_prev - m_new) + _dot(p, vh)

    @pl.when(ki < qi)
    def _():
        for h in range(n_heads):
            head_block(h, None)

    @pl.when(ki == qi)
    def _():
        causal = (lax.broadcasted_iota(jnp.int32, (tq, tk), 1) <= lax.broadcasted_iota(jnp.int32, (tq, tk), 0))
        for h in range(n_heads):
            head_block(h, causal)
        for pair in range(n_heads // 2):
            halves = []
            for h in (2 * pair, 2 * pair + 1):
                acc = acc_scr[h][...]
                halves.append(acc * (1.0 / acc[:, denom_lane(h):denom_lane(h) + 1]))
            out = jnp.where(own_lanes(0), halves[0], halves[1])
            o_ref[0, :, pair * LANES:(pair + 1) * LANES] = out.astype(o_ref.dtype)


def _attn_prompt(q, kb, vb, negc, *, n_heads, tile):
    nb, seqlen, aw = q.shape
    t = min(tile, seqlen)
    nq = seqlen // t
    pairs = [(i, j) for i in range(nq) for j in range(i + 1)]
    qi_tab = jnp.asarray([p[0] for p in pairs], jnp.int32)
    ki_tab = jnp.asarray([p[1] for p in pairs], jnp.int32)
    grid_spec = pltpu.PrefetchScalarGridSpec(
        num_scalar_prefetch=2, grid=(nb, len(pairs)),
        in_specs=[
            pl.BlockSpec((1, t, aw), lambda b, s, qi, ki: (b, qi[s], 0)),
            pl.BlockSpec((1, t, aw), lambda b, s, qi, ki: (b, ki[s], 0)),
            pl.BlockSpec((1, t, aw), lambda b, s, qi, ki: (b, ki[s], 0)),
            pl.BlockSpec((1, n_heads, t), lambda b, s, qi, ki: (b, 0, ki[s])),
        ],
        out_specs=pl.BlockSpec((1, t, aw), lambda b, s, qi, ki: (b, qi[s], 0)),
        scratch_shapes=[pltpu.VMEM((t, LANES), F32)] * (2 * n_heads),
    )
    return pl.pallas_call(
        functools.partial(_attn_kernel, n_heads=n_heads, tq=t, tk=t),
        grid_spec=grid_spec, out_shape=jax.ShapeDtypeStruct((nb, seqlen, aw), BF16),
        compiler_params=_params(("parallel", "arbitrary")), name="attn",
    )(qi_tab, ki_tab, q, kb, vb, negc)


def _dec_group(step, n_steps, nj, pt_ref, q_ref, kn_ref, vn_ref, lfn_ref, tri_ref, ones_ref,
               ck_hbm, cv_hbm, clf_hbm, o_ref, kbuf, vbuf, lfbuf, sem, qbd_scr, carry_scr, m_scr, l_scr, acc_scr,
               *, pp, n_heads, n_new, layer, n_pages, first, last, tail):
    b = step // nj
    j = lax.rem(step, nj)
    n_slots = kbuf.shape[0]
    slot = lax.rem(step, n_slots)
    rows = n_heads * n_new
    aw = n_heads * HEAD_DIM
    page = kbuf.shape[-1]

    def page_copies(seq, grp, half, lookup):
        out = []
        for i in range(pp):
            pg = pt_ref[seq, n_pages - 1 - (grp * pp + i)] if lookup else 0
            out.append(pltpu.make_async_copy(ck_hbm.at[layer, pg], kbuf.at[half, i], sem.at[0, half]))
            out.append(pltpu.make_async_copy(cv_hbm.at[layer, pg], vbuf.at[half, i], sem.at[1, half]))
            out.append(pltpu.make_async_copy(clf_hbm.at[layer, pg], lfbuf.at[half, i], sem.at[2, half]))
        return out

    def start_group(grp, half):
        for cp in page_copies(grp // nj, lax.rem(grp, nj), half, True):
            cp.start()

    ahead = n_slots - 1
    if first:
        @pl.when(step == 0)
        def _():
            for g in range(ahead):
                for cp in page_copies(g // nj, g % nj, g, True):
                    cp.start()

    def start_ahead():
        start_group(step + ahead, lax.rem(step + ahead, n_slots))

    if tail:
        pl.when(step + ahead < n_steps)(start_ahead)
    else:
        start_ahead()

    for cp in page_copies(b, j, slot, False):
        cp.wait()

    def pad_page(x):
        return jnp.concatenate([x, jnp.zeros((page - n_new, aw), F32)], axis=0).astype(BF16)

    def expand(r):
        return jnp.broadcast_to(r[:, None, :], (n_heads, n_new, r.shape[-1])).reshape(rows, r.shape[-1])

    def page_t(buf, i):
        return buf[slot, i].reshape(aw, page).astype(BF16)

    def start_sequence():
        qt = jnp.concatenate([q_ref[b].astype(F32)] * n_heads, axis=0)
        rowi = lax.broadcasted_iota(jnp.int32, (rows, aw), 0)
        lani = lax.broadcasted_iota(jnp.int32, (rows, aw), 1)
        qbd = jnp.where((lani // HEAD_DIM) == (rowi // n_new), qt, 0.0)
        qbd_scr[...] = qbd.astype(BF16)
        lfn = lfn_ref[b]
        run = jnp.zeros((1, LANES), F32)
        crow = []
        for t in range(n_new):
            run = run + lfn[t:t + 1, :]
            crow.append(run)
        cnew = jnp.concatenate(crow + [jnp.zeros((LANES - n_new, LANES), F32)], axis=0)
        bias = expand(-LOG2E * cnew.T[:n_heads, :])
        s = _dot_nt(qbd_scr[...], pad_page(kn_ref[b])) + bias
        rown = lax.broadcasted_iota(jnp.int32, (rows, LANES), 0)
        coln = lax.broadcasted_iota(jnp.int32, (rows, LANES), 1)
        s = jnp.where(coln <= (rown % n_new), s, NEG_INF)
        m = jnp.max(s, axis=1, keepdims=True)
        p = jnp.exp2(s - m)
        m_scr[...] = jnp.broadcast_to(m, m_scr.shape)
        l_scr[...] = jnp.broadcast_to(jnp.sum(p, axis=1, keepdims=True), l_scr.shape)
        acc_scr[...] = _dot(p.astype(BF16), pad_page(vn_ref[b]))
        carry_scr[...] = jnp.zeros(carry_scr.shape, F32)

    if first:
        pl.when(j == 0)(start_sequence)

    x = lfbuf[slot].reshape(pp * n_heads, page)
    within = _dot3(x, tri_ref[...])
    tot = _dot3(x, ones_ref[...])
    carry = carry_scr[...]
    qbd = qbd_scr[...]
    biases = []
    for i in range(pp):
        sl = slice(i * n_heads, (i + 1) * n_heads)
        biases.append(expand(LOG2E * (within[sl] + carry)))
        carry = carry + tot[sl]
    carry_scr[...] = carry
    s = jnp.concatenate([_dot(qbd, page_t(kbuf, i)) + biases[i] for i in range(pp)], axis=1)
    m_prev = m_scr[...]
    m_new = jnp.maximum(m_prev, jnp.max(s, axis=1, keepdims=True))
    alpha = jnp.exp2(m_prev - m_new)
    p = jnp.exp2(s - jnp.tile(m_new, (1, pp * page // LANES)))
    l_scr[...] = alpha * l_scr[...] + jnp.sum(p, axis=1, keepdims=True)
    m_scr[...] = m_new
    pv = _dot_nt(p[:, :page].astype(BF16), page_t(vbuf, 0))
    for i in range(1, pp):
        pv = pv + _dot_nt(p[:, i * page:(i + 1) * page].astype(BF16), page_t(vbuf, i))
    acc_scr[...] = acc_scr[...] * jnp.tile(alpha, (1, aw // LANES)) + pv

    def end_sequence():
        o = acc_scr[...] * jnp.tile(1.0 / l_scr[...], (1, aw // LANES))
        lano = lax.broadcasted_iota(jnp.int32, (n_new, aw), 1)
        out = jnp.zeros((n_new, aw), F32)
        for h in range(n_heads):
            out = jnp.where((lano // HEAD_DIM) == h, o[h * n_new:(h + 1) * n_new, :], out)
        o_ref[b] = out.astype(o_ref.dtype)

    return (lambda: pl.when(j == nj - 1)(end_sequence)) if last else None


def _dec_operands(q, k_new, v_new, lf_new, cache_k, cache_v, cache_lft, page_table, *, n_heads, pp):
    nb, n_new, aw = q.shape
    page = cache_k.shape[-1]
    rows = n_heads * n_new
    tri = jnp.asarray(np.tril(np.ones((page, page), np.float32), -1), BF16)
    ones = jnp.ones((page, page), BF16)
    whole = lambda a: pl.BlockSpec(a.shape, lambda i, pt: (0,) * a.ndim)
    hbm = pl.BlockSpec(memory_space=pl.ANY)
    small = [q, k_new, v_new, lf_new, tri, ones]
    operands = small + [cache_k, cache_v, cache_lft]
    in_specs = [whole(a) for a in small] + [hbm, hbm, hbm]
    out_shape = jax.ShapeDtypeStruct((nb, n_new, aw), BF16)
    out_spec = pl.BlockSpec((nb, n_new, aw), lambda i, pt: (0, 0, 0))
    scratch = [pltpu.VMEM((DEC_SLOTS, pp, n_heads, HEAD_DIM, page), F32),
               pltpu.VMEM((DEC_SLOTS, pp, n_heads, HEAD_DIM, page), F32),
               pltpu.VMEM((DEC_SLOTS, pp, n_heads, page), F32),
               pltpu.SemaphoreType.DMA((3, DEC_SLOTS)),
               pltpu.VMEM((rows, aw), BF16), pltpu.VMEM((n_heads, page), F32),
               pltpu.VMEM((rows, LANES), F32), pltpu.VMEM((rows, LANES), F32), pltpu.VMEM((rows, aw), F32)]
    return operands, in_specs, out_shape, out_spec, scratch


def _ssm_block_params(lam_re, lam_im, log_dt, b_re, b_im, c_re, c_im, d_skip, t_max):
    depth, g, p = lam_re.shape
    gpb = GROUPS_PER_BLOCK
    nj = g // gpb
    lam = lax.complex(lam_re.astype(F32), lam_im.astype(F32))
    lam_dt = lam * jnp.exp(log_dt.astype(F32))[..., None]
    lam_bar = jnp.exp(lam_dt)
    b_bar = ((lam_bar - 1.0) / lam)[..., None] * lax.complex(b_re.astype(F32), b_im.astype(F32))
    c = lax.complex(c_re.astype(F32), c_im.astype(F32))
    steps = jnp.arange(t_max + 1, dtype=F32).astype(jnp.complex64)
    pw = jnp.exp(lam_dt[:, :, None, :] * steps[None, None, :, None])
    pw = pw.reshape(depth, nj, gpb, t_max + 1, p).transpose(0, 1, 3, 2, 4).reshape(depth, nj, t_max + 1, gpb * p)
    eye = jnp.asarray(np.eye(gpb, dtype=bool))[None, None, :, None, :, None]

    def block_diag(a):
        a = jnp.where(eye, a[:, :, :, :, None, :], 0.0)
        return a.reshape(depth, nj, gpb * SSM_GROUP, gpb * p)

    bmat = block_diag(b_bar.transpose(0, 1, 3, 2).reshape(depth, nj, gpb, SSM_GROUP, p))
    cmat = block_diag(c.reshape(depth, nj, gpb, SSM_GROUP, p))
    ri = lambda z: (jnp.real(z), jnp.imag(z))
    dsk = d_skip.astype(F32).reshape(depth, 1, g * SSM_GROUP)
    return ri(bmat), ri(cmat), ri(pw), dsk


def _ssm_tables_kernel(bre_ref, bim_ref, cre_ref, cim_ref, pwr_ref, pwi_ref, mt_ref, bt_ref, ct_ref, *, t):
    bre, bim = bre_ref[...], bim_ref[...]
    cre, cim = cre_ref[...], cim_ref[...]
    half = lambda i: slice(i * LANES, (i + 1) * LANES)
    lo, hi = slice(0, HALF_STATE), slice(HALF_STATE, BLOCK_STATE)
    bd = []
    for e in range(t):
        pr, pi = pwr_ref[e:e + 1, :], pwi_ref[e:e + 1, :]
        xr = bre * pr - bim * pi
        xi = bre * pi + bim * pr
        s = t - 1 - e
        bt_ref[s // 2, half(s % 2), lo] = xr.astype(BF16)
        bt_ref[s // 2, half(s % 2), hi] = xi.astype(BF16)
        bd.append(_dot_nt_split(xr, cre) - _dot_nt_split(xi, cim))
        pr, pi = pwr_ref[e + 1:e + 2, :], pwi_ref[e + 1:e + 2, :]
        ct_ref[e // 2, half(e % 2), lo] = (cre * pr - cim * pi).astype(BF16)
        ct_ref[e // 2, half(e % 2), hi] = (-(cre * pi + cim * pr)).astype(BF16)
    zero = jnp.zeros((LANES, LANES), BF16)
    for d in range(t // 2):
        mt_ref[d, half(0), half(0)] = bd[2 * d].astype(BF16)
        mt_ref[d, half(0), half(1)] = bd[2 * d + 1].astype(BF16)
        mt_ref[d, half(1), half(0)] = bd[2 * d - 1].astype(BF16) if d else zero
        mt_ref[d, half(1), half(1)] = bd[2 * d].astype(BF16)


def _ssm_tables(block_params, t):
    (bre, bim), (cre, cim), (pwr, pwi), _ = block_params
    depth, nj = bre.shape[:2]
    nt = t // 2
    mat = pl.BlockSpec((None, None) + bre.shape[2:], lambda l, j: (l, j, 0, 0))
    pws = pl.BlockSpec((None, None) + pwr.shape[2:], lambda l, j: (l, j, 0, 0))
    tile = lambda a, b: pl.BlockSpec((None, None, nt, a, b), lambda l, j: (l, j, 0, 0, 0))
    shape = lambda a, b: jax.ShapeDtypeStruct((depth, nj, nt, a, b), BF16)
    return pl.pallas_call(
        functools.partial(_ssm_tables_kernel, t=t), grid=(depth, nj),
        in_specs=[mat, mat, mat, mat, pws, pws],
        out_specs=(tile(MXU_DIM, MXU_DIM), tile(MXU_DIM, BLOCK_STATE), tile(MXU_DIM, BLOCK_STATE)),
        out_shape=(shape(MXU_DIM, MXU_DIM), shape(MXU_DIM, BLOCK_STATE), shape(MXU_DIM, BLOCK_STATE)),
        compiler_params=_params(("parallel", "parallel")), name="ssm_tables",
    )(bre, bim, cre, cim, pwr, pwi)


def _ssm_scan_params(block_params, t):
    _, _, (pwr, pwi), dsk = block_params
    depth, nj = pwr.shape[:2]
    ar, ai = pwr[:, :, t], pwi[:, :, t]
    a1 = jnp.concatenate([ar, ar], axis=-1).reshape(depth, 1, nj * BLOCK_STATE)
    a2 = jnp.concatenate([-ai, ai], axis=-1).reshape(depth, 1, nj * BLOCK_STATE)
    dsk_c = jnp.broadcast_to(dsk.reshape(depth, nj, 1, LANES), (depth, nj, t, LANES)).reshape(depth, 1, nj * t * LANES)
    return a1, a2, dsk_c


def _ssm_kernel(u_ref, x0_ref, a1_ref, a2_ref, dsk_ref, bt_ref, mt_ref, ct_ref, y_ref, xl_ref,
                s_scr, xin_scr, x_scr, *, nb, cb, nt):
    rows = nb * cb
    n_lt = BLOCK_STATE // LANES
    u = u_ref[...].reshape(rows, nt * MXU_DIM)
    tile = lambda a: slice(a * MXU_DIM, (a + 1) * MXU_DIM)
    lt = lambda i: slice(i * LANES, (i + 1) * LANES)

    @pl.when(pl.program_id(1) == 0)
    def _():
        x_scr[...] = x0_ref[...]

    s = _dot(u[:, tile(0)], bt_ref[0])
    for b in range(1, nt):
        s = s + _dot(u[:, tile(b)], bt_ref[b])
    for i in range(n_lt):
        s_scr[i] = s[:, lt(i)]
    a1 = jnp.broadcast_to(a1_ref[...], (nb, BLOCK_STATE))
    a2 = jnp.broadcast_to(a2_ref[...], (nb, BLOCK_STATE))

    def step(c, x):
        sel = _rows_strided(c, nb, cb)
        for i in range(n_lt):
            xin_scr[i, sel, :] = x[:, lt(i)]
        swapped = jnp.concatenate([x[:, HALF_STATE:], x[:, :HALF_STATE]], axis=1)
        s_c = jnp.concatenate([s_scr[i, sel, :] for i in range(n_lt)], axis=1)
        return a1 * x + a2 * swapped + s_c

    x = lax.fori_loop(0, cb, step, x_scr[...], unroll=min(cb, SCAN_UNROLL))
    x_scr[...] = x
    xl_ref[...] = x
    xin = jnp.concatenate([xin_scr[i] for i in range(n_lt)], axis=1).astype(BF16)
    for a in range(nt):
        acc = _dot_nt(xin, ct_ref[a])
        for b in range(a + 1):
            acc = acc + _dot(u[:, tile(b)], mt_ref[a - b])
        acc = acc + dsk_ref[:, tile(a)] * u[:, tile(a)].astype(F32)
        if len(y_ref.shape) == 3:
            y_ref[:, :, tile(a)] = acc.reshape(nb, cb, MXU_DIM)
        else:
            y_ref[:, tile(a)] = acc


def _ssm(u_c, x0, tables, scan_params, layer, *, cb):
    mt, bt, ct = tables
    a1, a2, dsk = scan_params
    nj, nt = mt.shape[1], mt.shape[2]
    w = nt * MXU_DIM
    nb = u_c.shape[0]
    if u_c.ndim == 3:
        n_chunks = u_c.shape[1]
        cb = min(cb, n_chunks)
        io_spec = pl.BlockSpec((nb, cb, w), lambda j, r: (0, r, j))
    else:
        n_chunks = cb = 1
        io_spec = pl.BlockSpec((nb, w), lambda j, r: (0, j))
    state = pl.BlockSpec((nb, BLOCK_STATE), lambda j, r: (0, j))
    vec = lambda width: pl.BlockSpec((None, 1, width), lambda j, r: (layer, 0, j))
    tab = lambda a: pl.BlockSpec((None, None) + a.shape[2:], lambda j, r: (layer, j, 0, 0, 0))
    rows = nb * cb
    n_lt = BLOCK_STATE // LANES
    return pl.pallas_call(
        functools.partial(_ssm_kernel, nb=nb, cb=cb, nt=nt),
        grid=(nj, n_chunks // cb),
        in_specs=[io_spec, state, vec(BLOCK_STATE), vec(BLOCK_STATE), vec(w), tab(bt), tab(mt), tab(ct)],
        out_specs=(io_spec, state),
        out_shape=(jax.ShapeDtypeStruct(u_c.shape, F32),
                   jax.ShapeDtypeStruct((nb, nj * BLOCK_STATE), F32)),
        scratch_shapes=[pltpu.VMEM((n_lt, rows, LANES), F32), pltpu.VMEM((n_lt, rows, LANES), F32),
                        pltpu.VMEM((nb, BLOCK_STATE), F32)],
        compiler_params=_params(("parallel", "arbitrary")), name="ssm",
    )(u_c, x0, a1, a2, dsk, bt, mt, ct)


def _to_chunk_row(u):
    nb, t, w = u.shape
    return u.reshape(nb, t, w // LANES, LANES).transpose(0, 2, 1, 3).reshape(nb, t * w)


def _from_chunk_row(y, t):
    nb, tw = y.shape
    w = tw // t
    return y.reshape(nb, w // LANES, t, LANES).transpose(0, 2, 1, 3).reshape(nb, t, w)


def _pack_state(re, im):
    nb, g, p = re.shape
    nj = g // GROUPS_PER_BLOCK
    x = jnp.stack([re.astype(F32).reshape(nb, nj, HALF_STATE), im.astype(F32).reshape(nb, nj, HALF_STATE)], axis=2)
    return x.reshape(nb, nj * BLOCK_STATE)


def _unpack_state(x, n_groups):
    nb = x.shape[0]
    x = x.reshape(nb, n_groups // GROUPS_PER_BLOCK, 2, GROUPS_PER_BLOCK, SSM_STATE)
    return x[:, :, 0].reshape(nb, n_groups, SSM_STATE), x[:, :, 1].reshape(nb, n_groups, SSM_STATE)


FFN_WEIGHTS = ('w_glu', 'b_glu', 'g_attn_out', 'g_ssm_out', 'w_out', 'g_ffn', 'w_ff1', 'w_ff3', 'w_ff2', 'g_ple',
               'w_pg', 'b_pg', 'w_pe')
N_FFN_WEIGHTS = len(FFN_WEIGHTS)
N_DEC_OPERANDS = 9


def _out_ffn_kernel(*refs, ff_chunk, chunk, dec):
    if dec:
        pt_ref, refs = refs[0], refs[1:]
    h_ref, a_ref, y_ref, p_ref = refs[:4]
    (wglu_ref, bglu_ref, ga_ref, gs_ref, wo_ref, gffn_ref, w1_ref, w3_ref, w2_ref, gple_ref, wpg_ref, bpg_ref,
     wpe_ref) = refs[4:4 + N_FFN_WEIGHTS]
    refs = refs[4 + N_FFN_WEIGHTS:]
    if dec:
        dec_in, refs = refs[:N_DEC_OPERANDS], refs[N_DEC_OPERANDS:]
    o_ref, refs = refs[0], refs[1:]
    if dec:
        od_ref, refs = refs[0], refs[1:]
    if chunk:
        y_scr, refs = refs[0], refs[1:]
    d_ff = w1_ref.shape[1]
    n_ff = d_ff // ff_chunk

    hooks = {}
    if dec:
        per_step = dec['per_step']
        statics = {k: dec[k] for k in ('pp', 'n_heads', 'n_new', 'layer', 'n_pages')}
        base = pl.program_id(0) * per_step

        finish = []

        def side(k):
            fin = _dec_group(base + k, dec['total'], dec['nj'], pt_ref, *dec_in, od_ref, *refs, **statics,
                             first=k == 0, last=k == per_step - 1, tail=k + DEC_SLOTS - 1 >= per_step)
            if fin is not None:
                finish.append(fin)

        side(0)
        for k in range(1, per_step):
            hooks.setdefault(((k - 1) * n_ff) // max(per_step - 1, 1), []).append(k)

    if chunk:
        rows_c = y_ref.shape[0]
        for j in range(y_scr.shape[0]):
            for t in range(chunk):
                y_scr[j, pl.ds(t, rows_c, stride=chunk), :] = (
                    y_ref[:, (j * chunk + t) * LANES:(j * chunk + t + 1) * LANES])
        y = jnp.concatenate([y_scr[j] for j in range(y_scr.shape[0])], axis=1)
    else:
        y = y_ref[...]
    y = _gelu_tanh(y)
    ssm = y * _sigmoid(_dot(y.astype(BF16), wglu_ref[...]) + bglu_ref[...])
    na = _rms(a_ref[...].astype(F32), ga_ref[...]).astype(BF16)
    ns = _rms(ssm, gs_ref[...]).astype(BF16)
    aw = na.shape[1]
    h = h_ref[...] + _dot(na, wo_ref[:aw, :]) + _dot(ns, wo_ref[aw:, :])
    n2 = _rms(h, gffn_ref[...]).astype(BF16)
    ff = jnp.zeros(h.shape, F32)
    for c in range(n_ff):
        for k in hooks.get(c, ()):
            side(k)
        sl = slice(c * ff_chunk, (c + 1) * ff_chunk)
        a = _dot(n2, w1_ref[:, sl])
        b = _dot(n2, w3_ref[:, sl])
        ff = ff + _dot((a * _sigmoid(a) * b).astype(BF16), w2_ref[sl, :])
    h = h + ff
    n3 = _rms(h, gple_ref[...]).astype(BF16)
    gate = _sigmoid(_dot(n3, wpg_ref[...]) + bpg_ref[...])
    o_ref[...] = h + gate * _dot(p_ref[...].astype(BF16), wpe_ref[...])
    if dec:
        for fin in finish:
            fin()


def _out_ffn(h, attn, yssm, p_all, pw, layer, *, tm, chunk, dec_args=None):
    m, d = h.shape
    tm = min(tm, m)
    n_steps = m // tm
    sw = pw['w_glu'].shape[1]
    row = lambda w: pl.BlockSpec((tm, w), lambda i, *_: (i, 0))
    weights = [pw[k] for k in FFN_WEIGHTS]
    d_ff = pw['w_ff1'].shape[2]
    ff_chunk = FF_CHUNK if d_ff % FF_CHUNK == 0 else d_ff
    if chunk:
        y_spec = pl.BlockSpec((tm // chunk, chunk * sw), lambda i, *_: (i, 0))
        scratch = [pltpu.VMEM((sw // LANES, tm, LANES), F32)]
    else:
        y_spec = row(sw)
        scratch = []
    p_spec = pl.BlockSpec((None, tm, p_all.shape[2]), lambda i, *_: (layer, i, 0))
    in_specs = [row(d), row(attn.shape[1]), y_spec, p_spec] + [
        _layer_spec(a, layer, pipeline_mode=pl.Buffered(1)) for a in weights]
    operands = [h, attn, yssm, p_all, *weights]
    out_specs, out_shape = row(d), jax.ShapeDtypeStruct((m, d), F32)
    if dec_args is None:
        return pl.pallas_call(
            functools.partial(_out_ffn_kernel, ff_chunk=ff_chunk, chunk=chunk, dec=None),
            grid=(n_steps,), in_specs=in_specs, out_specs=out_specs, out_shape=out_shape, scratch_shapes=scratch,
            compiler_params=_params(("parallel",)), name="out_ffn",
        )(*operands)
    *dec_arrays, page_table, n_heads = dec_args
    n_pages = page_table.shape[1]
    nb_s, n_new = dec_arrays[0].shape[:2]
    pp = DEC_PAGES_PER_STEP
    while n_pages % pp or (nb_s * (n_pages // pp)) % n_steps or (n_pages // pp) % (nb_s * (n_pages // pp) // n_steps):
        pp //= 2
    nj = n_pages // pp
    total = nb_s * nj
    d_ops, d_specs, d_shape, d_spec, d_scratch = _dec_operands(*dec_arrays, page_table, n_heads=n_heads, pp=pp)
    dec = dict(per_step=total // n_steps, total=total, nj=nj, pp=pp, n_heads=n_heads, n_new=n_new, layer=layer,
               n_pages=n_pages)
    grid_spec = pltpu.PrefetchScalarGridSpec(
        num_scalar_prefetch=1, grid=(n_steps,), in_specs=in_specs + d_specs, out_specs=(out_specs, d_spec),
        scratch_shapes=scratch + d_scratch)
    return pl.pallas_call(
        functools.partial(_out_ffn_kernel, ff_chunk=ff_chunk, chunk=chunk, dec=dec),
        grid_spec=grid_spec, out_shape=(out_shape, d_shape),
        compiler_params=_params(("arbitrary",), FUSED_VMEM_LIMIT_BYTES), name="out_ffn_dec",
    )(page_table, *operands, *d_ops)


def _prep_weights(n_heads, aw, w):
    depth = w['w_in'].shape[0]
    row = lambda a: a.reshape(depth, 1, -1).astype(F32)
    w_in = w['w_in']
    a3 = 3 * aw
    f_cols = jnp.pad(w_in[:, :, a3:a3 + n_heads], ((0, 0), (0, 0), (0, LANES - n_heads)))
    w_in_k = jnp.concatenate([w_in[:, :, :a3], w_in[:, :, a3 + n_heads:], f_cols], axis=2).astype(BF16)
    head_of_lane = np.arange(MXU_DIM) // HEAD_DIM
    bf16 = lambda name: w[name].astype(BF16)
    return dict(
        g_mix=row(w['g_mix']), w_in=w_in_k,
        b_f=jnp.pad(row(w['b_f']), ((0, 0), (0, 0), (0, LANES - n_heads))),
        g_qk=jnp.concatenate([row(w['g_q']), row(w['g_k'])], axis=2),
        bd=jnp.asarray(head_of_lane[:, None] == head_of_lane[None, :], BF16),
        w_glu=bf16('w_glu'), b_glu=row(w['b_glu']), g_attn_out=row(w['g_attn_out']), g_ssm_out=row(w['g_ssm_out']),
        w_out=bf16('w_out'), g_ffn=row(w['g_ffn']), w_ff1=bf16('w_ff1'), w_ff3=bf16('w_ff3'), w_ff2=bf16('w_ff2'),
        g_ple=row(w['g_ple']), w_pg=bf16('w_pg'), b_pg=row(w['b_pg']), w_pe=bf16('w_pe'),
    )


def kernel(x_prompt, x_sample, p_prompt, p_sample, cache_k, cache_v, cache_logf, state_ssm_re, state_ssm_im, page_table, g_mix, w_in, b_f, g_q, g_k, lam_re, lam_im, log_dt, b_re, b_im, c_re, c_im, d_skip, w_glu, b_glu, g_attn_out, g_ssm_out, w_out, g_ffn, w_ff1, w_ff3, w_ff2, w_pe, g_ple, w_pg, b_pg):
    w = dict(g_mix=g_mix, w_in=w_in, b_f=b_f, g_q=g_q, g_k=g_k, w_glu=w_glu, b_glu=b_glu,
             g_attn_out=g_attn_out, g_ssm_out=g_ssm_out, w_out=w_out, g_ffn=g_ffn, w_ff1=w_ff1, w_ff3=w_ff3,
             w_ff2=w_ff2, w_pe=w_pe, g_ple=g_ple, w_pg=w_pg, b_pg=b_pg)
    depth = w_in.shape[0]
    nb, seqlen, d = x_prompt.shape
    db, n_new, _ = x_sample.shape
    n_heads = b_f.shape[1]
    aw = n_heads * HEAD_DIM
    n_groups = log_dt.shape[1]
    ck = jnp.transpose(cache_k, (0, 1, 3, 4, 2))
    cv = jnp.transpose(cache_v, (0, 1, 3, 4, 2))
    clft = jnp.swapaxes(cache_logf, 2, 3)
    chunk_p = min(PROMPT_CHUNK, seqlen)

    pw = _prep_weights(n_heads, aw, w)
    blocks = _ssm_block_params(lam_re, lam_im, log_dt, b_re, b_im, c_re, c_im, d_skip, max(chunk_p, n_new))
    tables_p, scan_p = _ssm_tables(blocks, chunk_p), _ssm_scan_params(blocks, chunk_p)
    tables_s, scan_s = _ssm_tables(blocks, n_new), _ssm_scan_params(blocks, n_new)
    pp_all = p_prompt.reshape(depth, nb * seqlen, -1)
    ps_all = p_sample.reshape(depth, db * n_new, -1)

    h_p = x_prompt
    h_s = x_sample.reshape(1, db * n_new, d)
    outs = {k: [] for k in ('k_p', 'v_p', 'lf_p', 're_p', 'im_p', 'k_s', 'v_s', 'lf_s', 're_s', 'im_s')}
    for l in range(depth):
        q, k, v, kb, vb, u_c, lf, lft = _in_proj(h_p, pw, l, n_heads=n_heads, tm=ROW_TILE, chunk=chunk_p)
        negc = _negcumsum(lft.reshape(nb * n_heads, seqlen)).reshape(nb, n_heads, seqlen)
        attn = _attn_prompt(q, kb, vb, negc, n_heads=n_heads, tile=ATTN_TILE)
        y_c, x_last = _ssm(u_c, jnp.zeros((nb, n_groups * 2 * SSM_STATE), F32), tables_p, scan_p, l,
                           cb=SSM_CHUNKS_PER_STEP)
        re, im = _unpack_state(x_last, n_groups)
        outs['k_p'].append(k.reshape(nb, seqlen, n_heads, HEAD_DIM))
        outs['v_p'].append(v.reshape(nb, seqlen, n_heads, HEAD_DIM))
        outs['lf_p'].append(jnp.swapaxes(lft, 1, 2))
        outs['re_p'].append(re)
        outs['im_p'].append(im)

        q, k, v, _, _, u, lf, _ = _in_proj(h_s, pw, l, n_heads=n_heads, tm=ROW_TILE, chunk=0)
        rs = lambda a: a.reshape(db, n_new, a.shape[-1])
        lf_pad = jnp.pad(rs(lf), ((0, 0), (0, 0), (0, LANES - n_heads)))
        h_p, attn_s = _out_ffn(h_p.reshape(nb * seqlen, d), attn.reshape(nb * seqlen, aw),
                               y_c.reshape(nb * seqlen // chunk_p, -1), pp_all, pw, l, tm=ROW_TILE, chunk=chunk_p,
                               dec_args=(rs(q), rs(k), rs(v), lf_pad, ck, cv, clft, page_table, n_heads))
        h_p = h_p.reshape(nb, seqlen, d)
        y_c, x_last = _ssm(_to_chunk_row(rs(u)), _pack_state(state_ssm_re[l], state_ssm_im[l]),
                           tables_s, scan_s, l, cb=1)
        h_s = _out_ffn(h_s.reshape(db * n_new, d), attn_s.reshape(db * n_new, aw),
                       _from_chunk_row(y_c, n_new).reshape(db * n_new, -1), ps_all, pw, l,
                       tm=ROW_TILE, chunk=0).reshape(1, db * n_new, d)
        re, im = _unpack_state(x_last, n_groups)
        outs['k_s'].append(k.reshape(db, n_new, n_heads, HEAD_DIM))
        outs['v_s'].append(v.reshape(db, n_new, n_heads, HEAD_DIM))
        outs['lf_s'].append(lf.reshape(db, n_new, n_heads))
        outs['re_s'].append(re)
        outs['im_s'].append(im)

    st = lambda name: jnp.stack(outs[name])
    return (h_p, h_s.reshape(db, n_new, d), st('k_p'), st('v_p'), st('lf_p'), st('re_p'), st('im_p'),
            st('k_s'), st('v_s'), st('lf_s'), st('re_s'), st('im_s'))
```
